```python
import math
import jax
import jax.numpy as jnp
from jax import lax
import numpy as np

D_MODEL = 1024
BATCH = 8
SEQ = 2048
DEPTH = 2
DEC_BATCH = 32
DEC_SEQ = 4
PAST_LEN = 8192
PAGE_SIZE = 128

ALPHA = (2 * DEPTH) ** 0.25
BETA = (8 * DEPTH) ** -0.25
LN_EPS = 1e-5
RMS_EPS = 1e-5
N_EVEN = (DEPTH + 1) // 2
N_ODD = DEPTH // 2
CONV_W = 4

RG_BLOCKS = 12
RG_BW = D_MODEL // 16
RG_WIDTH = RG_BLOCKS * RG_BW
RG_C = 8.0

B_HEADS = 12
B_HD = D_MODEL // 16
B_WIDTH = B_HEADS * B_HD
PATTERNS = ((128, 1), (512, 4), (2048, 16))
MAX_WINDOW = 2048
Q_BLOCK = 64
N_BUCKETS = 32
MAX_DISTANCE = MAX_WINDOW

EVEN_IN = 2 * RG_WIDTH + 3 * B_WIDTH
EVEN_MIX = RG_WIDTH + B_WIDTH

SSD_D_INNER = 2 * D_MODEL
SSD_P = 64
SSD_HEADS = SSD_D_INNER // SSD_P
SSD_N = 128
SSD_GROUPS = 4
SSD_HPG = SSD_HEADS // SSD_GROUPS
SSD_CONV_DIM = SSD_D_INNER + 2 * SSD_GROUPS * SSD_N
SSD_IN = SSD_D_INNER + SSD_CONV_DIM + SSD_HEADS
SSD_CHUNK = 128

N_EXPERTS = 32
TOP_K = 4
D_FF = D_MODEL
SWIGLU_LIMIT = 7.0
SWIGLU_ALPHA = 1.702
MOE_BLOCK = 512

kernel_name = 'hawk_longnet_mamba2_moe_decode_step'

F32 = jnp.float32


def layer_norm(x, g, b):
    xf = x.astype(F32)
    mu = jnp.mean(xf, -1, keepdims=True)
    xc = xf - mu
    var = jnp.mean(xc * xc, -1, keepdims=True)
    return (xc * lax.rsqrt(var + LN_EPS) * g.astype(F32) + b.astype(F32)).astype(x.dtype)


def causal_conv(x, buf, w, b):
    xx = jnp.concatenate([buf.astype(x.dtype), x], axis=1)
    y = lax.conv_general_dilated(xx, w[:, None, :].astype(x.dtype), window_strides=(1,), padding='VALID',
                                 dimension_numbers=('NWC', 'WIO', 'NWC'), feature_group_count=x.shape[-1])
    return y + b.astype(x.dtype), xx[:, -(CONV_W - 1):]


def _lin_combine(left, right):
    a1, b1 = left
    a2, b2 = right
    return a1 * a2, a2 * b1 + b2


def rg_lru(x, h0, w_a, b_a, w_i, b_i, lam):
    bsz, t, _ = x.shape
    xf = x.astype(F32)
    xb = xf.reshape(bsz, t, RG_BLOCKS, RG_BW)
    r = jax.nn.sigmoid(jnp.einsum('btni,nij->btnj', xb, w_a.astype(F32)) + b_a.astype(F32)).reshape(bsz, t, RG_WIDTH)
    i = jax.nn.sigmoid(jnp.einsum('btni,nij->btnj', xb, w_i.astype(F32)) + b_i.astype(F32)).reshape(bsz, t, RG_WIDTH)
    log_a = -RG_C * r * jax.nn.softplus(-lam.astype(F32))
    a = jnp.exp(log_a)
    u = jnp.sqrt(-jnp.expm1(2.0 * log_a)) * (i * xf)
    a_cum, h_cum = lax.associative_scan(_lin_combine, (a, u), axis=1)
    h = a_cum * h0.astype(F32)[:, None] + h_cum
    return h, h[:, -1]


def t5_bucket(dist):
    max_exact = N_BUCKETS // 2
    safe = np.maximum(dist, 1)
    large = max_exact + (np.log(safe / max_exact) / np.log(MAX_DISTANCE / max_exact)
                         * (N_BUCKETS - max_exact)).astype(np.int32)
    return np.where(dist < max_exact, dist, np.minimum(large, N_BUCKETS - 1)).astype(np.int32)


def dilated_attention(q, k, v, kv_buf, rel_bias):
    bsz, t, nh, hd = q.shape
    kv_new = jnp.stack([k, v], axis=2)
    n_past = kv_buf.shape[1]
    pad = MAX_WINDOW - n_past
    kv_all = jnp.concatenate([kv_buf.astype(kv_new.dtype), kv_new], axis=1)
    kv_pad = jnp.pad(kv_all, ((0, 0), (pad, 0), (0, 0), (0, 0), (0, 0)))
    valid = jnp.arange(MAX_WINDOW + t) >= pad
    blk = math.gcd(t, Q_BLOCK)
    nb = t // blk
    span = blk + MAX_WINDOW
    gathers = []
    for w, d in PATTERNS:
        dist = d * np.arange(w // d + 1)
        idx = MAX_WINDOW + np.arange(blk)[:, None] - dist[None, :]
        bias = rel_bias[t5_bucket(dist)].astype(F32).T
        gathers.append((idx, bias))
    scale = hd ** -0.5

    def block(args):
        qi, start = args
        kv_s = lax.dynamic_slice_in_dim(kv_pad, start, span, axis=1)
        ok_s = lax.dynamic_slice_in_dim(valid, start, span)
        ms, ss, os_ = [], [], []
        for idx, bias in gathers:
            kg = kv_s[:, idx]
            logits = jnp.einsum('bihd,bikhd->bihk', qi, kg[:, :, :, 0]).astype(F32) * scale + bias
            logits = jnp.where(ok_s[idx][None, :, None, :], logits, -jnp.inf)
            m = jnp.max(logits, -1, keepdims=True)
            pr = jnp.exp(logits - m)
            ss.append(jnp.sum(pr, -1))
            ms.append(m[..., 0])
            os_.append(jnp.einsum('bihk,bikhd->bihd', pr, kg[:, :, :, 1].astype(F32)))
        m_st = jnp.stack(ms)
        wts = jnp.exp(m_st - jnp.max(m_st, 0))
        num = jnp.sum(wts[..., None] * jnp.stack(os_), 0)
        den = jnp.sum(wts * jnp.stack(ss), 0)
        return (num / den[..., None]).astype(qi.dtype)

    qb = jnp.moveaxis(q.reshape(bsz, nb, blk, nh, hd), 1, 0)
    out = lax.map(block, (qb, jnp.arange(nb) * blk))
    out = jnp.moveaxis(out, 0, 1).reshape(bsz, t, nh, hd)
    return out, kv_new[:, -min(MAX_WINDOW, t):]


def mix_even(x, rg_conv_buf, rg_h0, kv_buf, rel_bias, w_in, conv_w, conv_b, w_a, b_a, w_i, b_i, lam, w_out):
    bsz, t, _ = x.shape
    proj = jnp.dot(x, w_in)
    xa, ga, q, k, v = jnp.split(proj, [RG_WIDTH, 2 * RG_WIDTH, 2 * RG_WIDTH + B_WIDTH, 2 * RG_WIDTH + 2 * B_WIDTH], axis=-1)
    xa, conv_new = causal_conv(xa, rg_conv_buf, conv_w, conv_b)
    h, h_last = rg_lru(xa, rg_h0, w_a, b_a, w_i, b_i, lam)
    ya = (jax.nn.gelu(ga.astype(F32)) * h).astype(x.dtype)
    yb, kv_new = dilated_attention(q.reshape(bsz, t, B_HEADS, B_HD), k.reshape(bsz, t, B_HEADS, B_HD),
                                   v.reshape(bsz, t, B_HEADS, B_HD), kv_buf, rel_bias)
    y = jnp.dot(jnp.concatenate([ya, yb.reshape(bsz, t, B_WIDTH)], axis=-1), w_out)
    return y, conv_new, h_last, kv_new


def ssd_scan(x, dt, a, bm, cm, h0):
    bsz, t = x.shape[:2]
    q = math.gcd(t, SSD_CHUNK)
    nc = t // q
    xg = x.reshape(bsz, nc, q, SSD_GROUPS, SSD_HPG, SSD_P)
    dtc = dt.reshape(bsz, nc, q, SSD_GROUPS, SSD_HPG)
    bc = bm.reshape(bsz, nc, q, SSD_GROUPS, SSD_N)
    cc = cm.reshape(bsz, nc, q, SSD_GROUPS, SSD_N)
    cum = jnp.cumsum(dtc * a.reshape(SSD_GROUPS, SSD_HPG), axis=2)
    causal = np.tril(np.ones((q, q), dtype=bool))[None, None, :, :, None, None]
    seg = cum[:, :, :, None] - cum[:, :, None, :]
    decay = jnp.exp(jnp.where(causal, seg, -jnp.inf))
    xdt = xg * dtc[..., None]
    cb = jnp.einsum('bcign,bcjgn->bcijg', cc, bc)
    y_diag = jnp.einsum('bcijgh,bcjghp->bcighp', cb[..., None] * decay, xdt)
    decay_end = jnp.exp(cum[:, :, -1:] - cum)
    states = jnp.einsum('bcjgn,bcjghp->bcghpn', bc, xdt * decay_end[..., None])
    chunk_decay = jnp.exp(cum[:, :, -1])

    def step(h, inp):
        s, cd = inp
        return h * cd[..., None, None] + s, h

    h0g = h0.reshape(bsz, SSD_GROUPS, SSD_HPG, SSD_P, SSD_N)
    h_last, h_in = lax.scan(step, h0g, (jnp.moveaxis(states, 1, 0), jnp.moveaxis(chunk_decay, 1, 0)))
    h_in = jnp.moveaxis(h_in, 0, 1)
    y_off = jnp.einsum('bcign,bcghpn->bcighp', cc, h_in) * jnp.exp(cum)[..., None]
    y = (y_diag + y_off).reshape(bsz, t, SSD_HEADS, SSD_P)
    return y, h_last.reshape(bsz, SSD_HEADS, SSD_P, SSD_N)


def mix_odd(x, conv_buf, h0, w_in, conv_w, conv_b, dt_bias, a_log, d_skip, norm_w, w_out):
    bsz, t, _ = x.shape
    proj = jnp.dot(x, w_in)
    z, xbc, dt = jnp.split(proj, [SSD_D_INNER, SSD_D_INNER + SSD_CONV_DIM], axis=-1)
    xbc, conv_new = causal_conv(xbc, conv_buf, conv_w, conv_b)
    xbc = jax.nn.silu(xbc.astype(F32))
    xs, bm, cm = jnp.split(xbc, [SSD_D_INNER, SSD_D_INNER + SSD_GROUPS * SSD_N], axis=-1)
    xs = xs.reshape(bsz, t, SSD_HEADS, SSD_P)
    bm = bm.reshape(bsz, t, SSD_GROUPS, SSD_N)
    cm = cm.reshape(bsz, t, SSD_GROUPS, SSD_N)
    dt = jax.nn.softplus(dt.astype(F32) + dt_bias.astype(F32))
    a = -jnp.exp(a_log.astype(F32))
    y, h_last = ssd_scan(xs, dt, a, bm, cm, h0.astype(F32))
    y = y + d_skip.astype(F32)[:, None] * xs
    g = y.reshape(bsz, t, SSD_GROUPS, -1) * jax.nn.silu(z.astype(F32)).reshape(bsz, t, SSD_GROUPS, -1)
    g = g * lax.rsqrt(jnp.mean(g * g, -1, keepdims=True) + RMS_EPS)
    g = g.reshape(bsz, t, SSD_D_INNER) * norm_w.astype(F32)
    return jnp.dot(g.astype(x.dtype), w_out), conv_new, h_last


def clamped_swiglu(h):
    glu = jnp.minimum(h[..., ::2], SWIGLU_LIMIT)
    lin = jnp.clip(h[..., 1::2], -SWIGLU_LIMIT, SWIGLU_LIMIT)
    return glu * jax.nn.sigmoid(SWIGLU_ALPHA * glu) * (lin + 1.0)


def moe(x, w_r, b_r, w1, b1, w2, b2):
    shp = x.shape
    xf = x.reshape(-1, shp[-1])
    n = xf.shape[0]
    logits = jnp.dot(xf.astype(F32), w_r.astype(F32)) + b_r.astype(F32)
    top_v, top_i = lax.top_k(logits, TOP_K)
    gate = jax.nn.softmax(top_v, axis=-1)
    gate_full = jnp.einsum('nk,nke->ne', gate, jax.nn.one_hot(top_i, N_EXPERTS, dtype=F32))
    blk = min(MOE_BLOCK, n)
    pad = (-n) % blk
    xp = jnp.pad(xf, ((0, pad), (0, 0))).reshape(-1, blk, shp[-1])
    gp = jnp.pad(gate_full, ((0, pad), (0, 0))).reshape(-1, blk, N_EXPERTS)

    def expert_block(args):
        xb, gb = args
        h = jnp.einsum('nd,edf->nef', xb, w1) + b1
        o = jnp.einsum('nef,efd->ned', clamped_swiglu(h), w2) + b2
        return jnp.einsum('ne,ned->nd', gb.astype(o.dtype), o)

    out = lax.map(expert_block, (xp, gp)).reshape(-1, shp[-1])[:n]
    return out.reshape(shp).astype(x.dtype)


def trunk(x, rg_conv, rg_h, swa_kv, ssd_conv, ssd_h, p):
    n_rg_conv, n_rg_h, n_kv, n_ssd_conv, n_ssd_h = [], [], [], [], []
    for layer in range(DEPTH):
        j = layer // 2
        if layer % 2 == 0:
            y, c, h, kv = mix_even(x, rg_conv[j], rg_h[j], swa_kv[j], p['rel_bias'], p['w_in_mix'][j],
                                   p['rg_conv_w'][j], p['rg_conv_b'][j], p['rg_w_a'][j], p['rg_b_a'][j],
                                   p['rg_w_i'][j], p['rg_b_i'][j], p['rg_lambda'][j], p['w_out_mix'][j])
            n_rg_conv.append(c)
            n_rg_h.append(h)
            n_kv.append(kv)
        else:
            y, c, h = mix_odd(x, ssd_conv[j], ssd_h[j], p['ssd_w_in'][j], p['ssd_conv_w'][j], p['ssd_conv_b'][j],
                              p['ssd_dt_bias'][j], p['ssd_a_log'][j], p['ssd_d'][j], p['ssd_norm_w'][j],
                              p['ssd_w_out'][j])
            n_ssd_conv.append(c)
            n_ssd_h.append(h)
        x = layer_norm(ALPHA * x + y, p['ln_g'][layer, 0], p['ln_b'][layer, 0])
        f = moe(x, p['router_w'][layer], p['router_b'][layer], p['exp_w1'][layer], p['exp_b1'][layer],
                p['exp_w2'][layer], p['exp_b2'][layer])
        x = layer_norm(ALPHA * x + f, p['ln_g'][layer, 1], p['ln_b'][layer, 1])
    return x, jnp.stack(n_rg_conv), jnp.stack(n_rg_h), jnp.stack(n_kv), jnp.stack(n_ssd_conv), jnp.stack(n_ssd_h)


def setup_inputs(seed: int = 0) -> dict:
    key = jax.random.key(seed)
    ks = iter(jax.random.split(key, 48))

    def nrm(shape, scale):
        return scale * jax.random.normal(next(ks), shape, F32)

    wbuf = min(MAX_WINDOW, PAST_LEN)
    u = jax.random.uniform(next(ks), (N_EVEN, RG_WIDTH), F32, 0.9, 0.999)
    a0 = u ** (1.0 / RG_C)
    rg_lambda = jnp.log(a0) - jnp.log1p(-a0)
    dt0 = jnp.exp(jax.random.uniform(next(ks), (N_ODD, SSD_HEADS), F32, math.log(1e-3), math.log(1e-1)))
    ssd_dt_bias = dt0 + jnp.log(-jnp.expm1(-dt0))
    ssd_a_log = jnp.log(jax.random.uniform(next(ks), (N_ODD, SSD_HEADS), F32, 1.0, 16.0))
    return {
        'x_prompt': nrm((BATCH, SEQ, D_MODEL), 1.0),
        'x_sample': nrm((DEC_BATCH, DEC_SEQ, D_MODEL), 1.0),
        'state_rglru_conv': nrm((N_EVEN, DEC_BATCH, CONV_W - 1, RG_WIDTH), 1.0),
        'state_rglru_h': nrm((N_EVEN, DEC_BATCH, RG_WIDTH), 1.0),
        'cache_swa_kv': nrm((N_EVEN, DEC_BATCH, wbuf, 2, B_HEADS, B_HD), 1.0),
        'state_ssd_conv': nrm((N_ODD, DEC_BATCH, CONV_W - 1, SSD_CONV_DIM), 1.0),
        'state_ssd_h': nrm((N_ODD, DEC_BATCH, SSD_HEADS, SSD_P, SSD_N), 0.1),
        'rel_bias': nrm((N_BUCKETS, B_HEADS), 0.2),
        'w_in_mix': nrm((N_EVEN, D_MODEL, EVEN_IN), D_MODEL ** -0.5),
        'rg_conv_w': nrm((N_EVEN, CONV_W, RG_WIDTH), CONV_W ** -0.5),
        'rg_conv_b': nrm((N_EVEN, RG_WIDTH), 0.01),
        'rg_w_a': nrm((N_EVEN, RG_BLOCKS, RG_BW, RG_BW), RG_BW ** -0.5),
        'rg_b_a': nrm((N_EVEN, RG_BLOCKS, RG_BW), 0.01),
        'rg_w_i': nrm((N_EVEN, RG_BLOCKS, RG_BW, RG_BW), RG_BW ** -0.5),
        'rg_b_i': nrm((N_EVEN, RG_BLOCKS, RG_BW), 0.01),
        'rg_lambda': rg_lambda,
        'w_out_mix': nrm((N_EVEN, EVEN_MIX, D_MODEL), BETA * EVEN_MIX ** -0.5),
        'ssd_w_in': nrm((N_ODD, D_MODEL, SSD_IN), D_MODEL ** -0.5),
        'ssd_conv_w': nrm((N_ODD, CONV_W, SSD_CONV_DIM), CONV_W ** -0.5),
        'ssd_conv_b': nrm((N_ODD, SSD_CONV_DIM), 0.01),
        'ssd_dt_bias': ssd_dt_bias,
        'ssd_a_log': ssd_a_log,
        'ssd_d': 1.0 + nrm((N_ODD, SSD_HEADS), 0.1),
        'ssd_norm_w': 1.0 + nrm((N_ODD, SSD_D_INNER), 0.05),
        'ssd_w_out': nrm((N_ODD, SSD_D_INNER, D_MODEL), BETA * SSD_D_INNER ** -0.5),
        'ln_g': 1.0 + nrm((DEPTH, 2, D_MODEL), 0.05),
        'ln_b': nrm((DEPTH, 2, D_MODEL), 0.01),
        'router_w': nrm((DEPTH, D_MODEL, N_EXPERTS), D_MODEL ** -0.5),
        'router_b': nrm((DEPTH, N_EXPERTS), 0.01),
        'exp_w1': nrm((DEPTH, N_EXPERTS, D_MODEL, 2 * D_FF), D_MODEL ** -0.5),
        'exp_b1': nrm((DEPTH, N_EXPERTS, 2 * D_FF), 0.01),
        'exp_w2': nrm((DEPTH, N_EXPERTS, D_FF, D_MODEL), BETA * D_FF ** -0.5),
        'exp_b2': nrm((DEPTH, N_EXPERTS, D_MODEL), 0.01),
    }


def reference(x_prompt, x_sample, state_rglru_conv, state_rglru_h, cache_swa_kv, state_ssd_conv, state_ssd_h,
              rel_bias, w_in_mix, rg_conv_w, rg_conv_b, rg_w_a, rg_b_a, rg_w_i, rg_b_i, rg_lambda, w_out_mix,
              ssd_w_in, ssd_conv_w, ssd_conv_b, ssd_dt_bias, ssd_a_log, ssd_d, ssd_norm_w, ssd_w_out,
              ln_g, ln_b, router_w, router_b, exp_w1, exp_b1, exp_w2, exp_b2):
    p = dict(rel_bias=rel_bias, w_in_mix=w_in_mix, rg_conv_w=rg_conv_w, rg_conv_b=rg_conv_b, rg_w_a=rg_w_a,
             rg_b_a=rg_b_a, rg_w_i=rg_w_i, rg_b_i=rg_b_i, rg_lambda=rg_lambda, w_out_mix=w_out_mix,
             ssd_w_in=ssd_w_in, ssd_conv_w=ssd_conv_w, ssd_conv_b=ssd_conv_b, ssd_dt_bias=ssd_dt_bias,
             ssd_a_log=ssd_a_log, ssd_d=ssd_d, ssd_norm_w=ssd_norm_w, ssd_w_out=ssd_w_out, ln_g=ln_g, ln_b=ln_b,
             router_w=router_w, router_b=router_b, exp_w1=exp_w1, exp_b1=exp_b1, exp_w2=exp_w2, exp_b2=exp_b2)
    bp = x_prompt.shape[0]
    dt_ = x_prompt.dtype
    y_prompt, p_rg_conv, p_rg_h, p_kv, p_ssd_conv, p_ssd_h = trunk(
        x_prompt,
        jnp.zeros((N_EVEN, bp, CONV_W - 1, RG_WIDTH), dt_),
        jnp.zeros((N_EVEN, bp, RG_WIDTH), dt_),
        jnp.zeros((N_EVEN, bp, 0, 2, B_HEADS, B_HD), dt_),
        jnp.zeros((N_ODD, bp, CONV_W - 1, SSD_CONV_DIM), dt_),
        jnp.zeros((N_ODD, bp, SSD_HEADS, SSD_P, SSD_N), dt_),
        p)
    y_sample, s_rg_conv, s_rg_h, s_kv, s_ssd_conv, s_ssd_h = trunk(
        x_sample, state_rglru_conv, state_rglru_h, cache_swa_kv, state_ssd_conv, state_ssd_h, p)
    return (y_prompt, y_sample, p_rg_conv, p_rg_h, p_kv, p_ssd_conv, p_ssd_h,
            s_rg_conv, s_rg_h, s_kv, s_ssd_conv, s_ssd_h)
```

```python
import functools
import math

import numpy as np
import jax
import jax.numpy as jnp
from jax import lax
from jax.experimental import pallas as pl
from jax.experimental.pallas import tpu as pltpu

F32 = jnp.float32
BF16 = jnp.bfloat16
I32 = jnp.int32

D_MODEL = 1024
DEPTH = 2
ALPHA = (2 * DEPTH) ** 0.25
LN_EPS = 1e-5
RMS_EPS = 1e-5
CONV_W = 4
RG_BLOCKS = 12
RG_BW = 64
RG_WIDTH = RG_BLOCKS * RG_BW
RG_C = 8.0
B_HEADS = 12
B_HD = 64
B_WIDTH = B_HEADS * B_HD
PATTERNS = ((128, 1), (512, 4), (2048, 16))
MAX_WINDOW = 2048
N_BUCKETS = 32
MAX_DISTANCE = MAX_WINDOW
EVEN_IN = 2 * RG_WIDTH + 3 * B_WIDTH
SSD_D_INNER = 2 * D_MODEL
SSD_P = 64
SSD_HEADS = SSD_D_INNER // SSD_P
SSD_N = 128
SSD_GROUPS = 4
SSD_GN = SSD_GROUPS * SSD_N
SSD_CONV_DIM = SSD_D_INNER + 2 * SSD_GN
SSD_MAIN = SSD_D_INNER + SSD_CONV_DIM
N_EXPERTS = 32
TOP_K = 4
D_FF = D_MODEL
SWIGLU_LIMIT = 7.0
SWIGLU_ALPHA = 1.702

LANES = 128
VMEM_LIMIT = 56 * 1024 * 1024

SSD_Q = 128
ATT_BLK = 256
MOE_ROWS = 256
NEG_INF = float("-inf")


def _cp(*sem):
    return pltpu.CompilerParams(dimension_semantics=sem, vmem_limit_bytes=VMEM_LIMIT)


def _ln_rows(v, g, b):
    mu = jnp.mean(v, -1, keepdims=True)
    xc = v - mu
    var = jnp.mean(xc * xc, -1, keepdims=True)
    return xc * lax.rsqrt(var + LN_EPS) * g + b


def _sigmoid(x):
    return 1.0 / (1.0 + jnp.exp(-x))


def _silu(x):
    return x * _sigmoid(x)


def _softplus(x):
    return jnp.maximum(x, 0.0) + jnp.log1p(jnp.exp(-jnp.abs(x)))


def _gelu_tanh(x):
    return 0.5 * x * (1.0 + jnp.tanh(math.sqrt(2.0 / math.pi) * (x + 0.044715 * (x * x * x))))


def _row_tile(n):
    return min(n, 256)


def _inproj_even_kernel(x_ref, w_ref, xa_ref, ga_ref, q_ref, kv_ref):
    xb = x_ref[...].astype(BF16)
    c0, c1, c2 = RG_WIDTH, 2 * RG_WIDTH, 2 * RG_WIDTH + B_WIDTH
    xa_ref[...] = jnp.dot(xb, w_ref[:, 0:c0], preferred_element_type=F32)
    ga_ref[...] = jnp.dot(xb, w_ref[:, c0:c1], preferred_element_type=F32)
    q_ref[...] = jnp.dot(xb, w_ref[:, c1:c2], preferred_element_type=F32)
    kv_ref[...] = jnp.dot(xb, w_ref[:, c2:EVEN_IN], preferred_element_type=F32)


def _inproj_even(x_all, row0, n, w):
    tm = _row_tile(n)
    off = row0 // tm
    widths = (RG_WIDTH, RG_WIDTH, B_WIDTH, 2 * B_WIDTH)
    return pl.pallas_call(
        _inproj_even_kernel,
        grid=(n // tm,),
        in_specs=[pl.BlockSpec((tm, D_MODEL), lambda i: (i + off, 0)),
                  pl.BlockSpec((D_MODEL, EVEN_IN), lambda i: (0, 0))],
        out_specs=[pl.BlockSpec((tm, c), lambda i: (i, 0)) for c in widths],
        out_shape=[jax.ShapeDtypeStruct((n, c), F32) for c in widths],
        compiler_params=_cp("parallel"),
        name="inproj_even",
    )(x_all, w)


def _rglru_kernel(xa_ref, ga_ref, cbuf_ref, h0_ref, cw_ref, cb_ref, wg_ref, bg_ref, c_ref,
                  ya_ref, cnew_ref, hlast_ref, xx, a_s, u_s, hc, *, tt):
    j = pl.program_id(1)
    hist = CONV_W - 1
    base = 8

    @pl.when(j == 0)
    def _():
        xx[base - hist:base, :] = cbuf_ref[...]
        hc[...] = h0_ref[...]

    @pl.when(j > 0)
    def _():
        xx[base - hist:base, :] = xx[base + tt - hist:base + tt, :]

    xx[base:base + tt, :] = xa_ref[...]
    conv = cb_ref[...]
    for k in range(CONV_W):
        conv = conv + cw_ref[k:k + 1, :] * xx[base - hist + k:base - hist + k + tt, :]
    gates = jnp.dot(conv.astype(BF16), wg_ref[...], preferred_element_type=F32) + bg_ref[...]
    r = _sigmoid(gates[:, :RG_WIDTH])
    ig = _sigmoid(gates[:, RG_WIDTH:])
    log_a = c_ref[...] * r
    a = jnp.exp(log_a)
    a_s[...] = a
    u_s[...] = jnp.sqrt(-jnp.tanh(log_a) * (a * a + 1.0)) * (ig * conv)

    def step(t, h):
        h = a_s[pl.ds(t, 1), :] * h + u_s[pl.ds(t, 1), :]
        u_s[pl.ds(t, 1), :] = h
        return h

    h = lax.fori_loop(0, tt, step, hc[...], unroll=min(tt, 8))
    hc[...] = h
    ya_ref[...] = _gelu_tanh(ga_ref[...]) * u_s[...]

    @pl.when(j == pl.num_programs(1) - 1)
    def _():
        hlast_ref[...] = h
        cnew_ref[...] = xx[base + tt - hist:base + tt, :]


def _rglru(xa, ga, cbuf, h0, cw, cb, wg, bg, cvec):
    bsz, t, c = xa.shape
    tt = min(t, 256)
    row = lambda b, j: (b, j, 0)
    per_b = lambda b, j: (b, 0, 0)
    fixed = lambda b, j: (0, 0)
    return pl.pallas_call(
        functools.partial(_rglru_kernel, tt=tt),
        grid=(bsz, t // tt),
        in_specs=[pl.BlockSpec((None, tt, c), row), pl.BlockSpec((None, tt, c), row),
                  pl.BlockSpec((None, CONV_W - 1, c), per_b), pl.BlockSpec((None, 1, c), per_b),
                  pl.BlockSpec((CONV_W, c), fixed), pl.BlockSpec((1, c), fixed),
                  pl.BlockSpec((c, 2 * c), fixed), pl.BlockSpec((1, 2 * c), fixed),
                  pl.BlockSpec((1, c), fixed)],
        out_specs=[pl.BlockSpec((None, tt, c), row), pl.BlockSpec((None, CONV_W - 1, c), per_b),
                   pl.BlockSpec((None, 1, c), per_b)],
        out_shape=[jax.ShapeDtypeStruct((bsz, t, c), F32),
                   jax.ShapeDtypeStruct((bsz, CONV_W - 1, c), F32),
                   jax.ShapeDtypeStruct((bsz, 1, c), F32)],
        scratch_shapes=[pltpu.VMEM((8 + tt, c), F32), pltpu.VMEM((tt, c), F32),
                        pltpu.VMEM((tt, c), F32), pltpu.VMEM((1, c), F32)],
        compiler_params=_cp("parallel", "arbitrary"),
        name="rglru",
    )(xa, ga, cbuf, h0, cw, cb, wg, bg, cvec)


def _t5_bucket(dist):
    max_exact = N_BUCKETS // 2
    safe = np.maximum(dist, 1)
    large = max_exact + (np.log(safe / max_exact) / np.log(MAX_DISTANCE / max_exact)
                         * (N_BUCKETS - max_exact)).astype(np.int32)
    return np.where(dist < max_exact, dist, np.minimum(large, N_BUCKETS - 1)).astype(np.int32)


def _distance_bias(rel_bias, max_d):
    d = np.arange(max_d + 1)
    mult = np.zeros(max_d + 1, np.float64)
    for w, dil in PATTERNS:
        mult += ((d % dil == 0) & (d <= w)).astype(np.float64)
    logm = np.where(mult > 0, np.log(np.maximum(mult, 1.0)), -np.inf).astype(np.float32)
    vals = rel_bias.astype(F32)[_t5_bucket(d)].T + jnp.asarray(logm)[None, :]
    return jnp.concatenate([vals, jnp.full((vals.shape[0], 1), NEG_INF, F32)], axis=1)


def _attn_prompt_kernel(q_ref, k_ref, v_ref, tab_ref, o_ref, *, blk):
    qi = pl.program_id(2)
    lo = lax.broadcasted_iota(I32, (blk, LANES), 1) < B_HD
    q2 = q_ref[...] * (B_HD ** -0.5)
    qh = (jnp.where(lo, q2, 0.0).astype(BF16), jnp.where(lo, 0.0, q2).astype(BF16))
    nt = (((1,), (1,)), ((), ()))

    def body(d, carry):
        m0, l0, m1, l1, acc = carry
        start = pl.multiple_of((qi - d) * blk, blk)
        k = k_ref[pl.ds(start, blk), :].astype(BF16)
        v = v_ref[pl.ds(start, blk), :].astype(BF16)
        new = []
        for h, (m, l) in enumerate(((m0, l0), (m1, l1))):
            s = lax.dot_general(qh[h], k, nt, preferred_element_type=F32) + tab_ref[h, d]
            mn = jnp.maximum(m, jnp.max(s, axis=1, keepdims=True))
            p = jnp.exp(s - mn)
            al = jnp.exp(m - mn)
            ln = al * l + jnp.sum(p, axis=1, keepdims=True)
            pv = jnp.dot(p.astype(BF16), v, preferred_element_type=F32)
            new.append((mn, ln, al, pv))
        (m0, l0, a0, pv0), (m1, l1, a1, pv1) = new
        acc = jnp.where(lo, a0, a1) * acc + jnp.where(lo, pv0, pv1)
        return m0, l0, m1, l1, acc

    minf = jnp.full((blk, 1), NEG_INF, F32)
    zero = jnp.zeros((blk, 1), F32)
    m0, l0, m1, l1, acc = lax.fori_loop(0, qi + 1, body, (minf, zero, minf, zero, jnp.zeros((blk, LANES), F32)))
    o_ref[...] = acc / jnp.where(lo, l0, l1)


def _attn_prompt(q, kv, rel_bias):
    bsz, t, _ = q.shape
    blk = min(ATT_BLK, t)
    nq = t // blk
    npair = B_HEADS // 2
    vals = _distance_bias(rel_bias, t - 1)
    idx = (np.arange(nq)[:, None, None] * blk + np.arange(blk)[None, :, None] - np.arange(blk)[None, None, :])
    idx = np.where(idx < 0, t, idx).astype(np.int32)
    table = jnp.take(vals, jnp.asarray(idx.reshape(-1)), axis=1).reshape(npair, 2, nq, blk, blk)
    return pl.pallas_call(
        functools.partial(_attn_prompt_kernel, blk=blk),
        grid=(npair, bsz, nq),
        in_specs=[pl.BlockSpec((None, blk, LANES), lambda p, b, i: (b, i, p)),
                  pl.BlockSpec((None, t, LANES), lambda p, b, i: (b, 0, p)),
                  pl.BlockSpec((None, t, LANES), lambda p, b, i: (b, 0, npair + p)),
                  pl.BlockSpec((None, 2, nq, blk, blk), lambda p, b, i: (p, 0, 0, 0, 0))],
        out_specs=pl.BlockSpec((None, blk, LANES), lambda p, b, i: (b, i, p)),
        out_shape=jax.ShapeDtypeStruct((bsz, t, B_WIDTH), F32),
        compiler_params=_cp("parallel", "parallel", "parallel"),
        name="attn_prompt",
    )(q, kv, kv, table)


ATT_S_ROWS = 16


def _attn_sample_kernel(qbd_ref, kc_ref, vc_ref, kn_ref, vn_ref, tabc_ref, tabn_ref, mask_ref, o_ref,
                        m_s, l_s, acc_s, *, t_new):
    c = pl.program_id(1)
    rows = t_new * ATT_S_ROWS
    nt = (((1,), (1,)), ((), ()))

    @pl.when(c == 0)
    def _():
        m_s[...] = jnp.full((rows, 1), NEG_INF, F32)
        l_s[...] = jnp.zeros((rows, 1), F32)
        acc_s[...] = jnp.zeros((rows, B_WIDTH), F32)

    qf = qbd_ref[...] * (B_HD ** -0.5)
    qb = qf.astype(BF16)
    s = lax.dot_general(qb, kc_ref[...].astype(BF16), nt, preferred_element_type=F32) + tabc_ref[...]
    m = m_s[...]
    mn = jnp.maximum(m, jnp.max(s, axis=1, keepdims=True))
    last = c == pl.num_programs(1) - 1

    sn = [jnp.sum(qf * kn_ref[j:j + 1, :], axis=1, keepdims=True) + tabn_ref[:, j:j + 1] for j in range(t_new)]
    sn = [jnp.where(last, x, NEG_INF) for x in sn]
    for x in sn:
        mn = jnp.maximum(mn, x)
    safe = jnp.where(mn == NEG_INF, 0.0, mn)
    p = jnp.exp(s - safe)
    al = jnp.exp(m - safe)
    l = al * l_s[...] + jnp.sum(p, axis=1, keepdims=True)
    acc = al * acc_s[...] + jnp.dot(p.astype(BF16), vc_ref[...].astype(BF16), preferred_element_type=F32)
    for j, x in enumerate(sn):
        pj = jnp.exp(x - safe)
        l = l + pj
        acc = acc + pj * vn_ref[j:j + 1, :]
    m_s[...] = mn
    l_s[...] = l
    acc_s[...] = acc

    @pl.when(last)
    def _():
        om = (acc / l) * mask_ref[...]
        for t in range(t_new):
            o_ref[t:t + 1, :] = jnp.sum(om[t * ATT_S_ROWS:(t + 1) * ATT_S_ROWS, :], axis=0, keepdims=True)


def _attn_sample(q, cache, kv_new, rel_bias):
    bsz, t_new, _ = q.shape
    w = cache.shape[1]
    rows = t_new * ATT_S_ROWS
    vals = _distance_bias(rel_bias, MAX_WINDOW)
    vals = jnp.concatenate([vals, jnp.zeros((ATT_S_ROWS - B_HEADS, vals.shape[1]), F32)], axis=0)
    tq = np.repeat(np.arange(t_new), ATT_S_ROWS)
    hq = np.tile(np.arange(ATT_S_ROWS), t_new)
    dist_c = w + tq[:, None] - np.arange(w)[None, :]
    dist_c = np.where(dist_c > MAX_WINDOW, MAX_WINDOW + 1, dist_c)
    dist_n = tq[:, None] - np.arange(t_new)[None, :]
    dist_n = np.where(dist_n < 0, MAX_WINDOW + 1, dist_n)
    tabc = vals[jnp.asarray(hq)[:, None], jnp.asarray(dist_c)]
    tabn = vals[jnp.asarray(hq)[:, None], jnp.asarray(dist_n)]
    mask = (hq[:, None] == (np.arange(B_WIDTH)[None, :] // B_HD)).astype(np.float32)
    qbd = (q[:, :, None, :] * jnp.asarray(mask.reshape(t_new, ATT_S_ROWS, B_WIDTH))[None]).reshape(bsz, rows, B_WIDTH)
    kc = 1024
    return pl.pallas_call(
        functools.partial(_attn_sample_kernel, t_new=t_new),
        grid=(bsz, w // kc),
        in_specs=[pl.BlockSpec((None, rows, B_WIDTH), lambda b, c: (b, 0, 0)),
                  pl.BlockSpec((None, kc, B_WIDTH), lambda b, c: (b, c, 0)),
                  pl.BlockSpec((None, kc, B_WIDTH), lambda b, c: (b, c, 1)),
                  pl.BlockSpec((None, t_new, B_WIDTH), lambda b, c: (b, 0, 0)),
                  pl.BlockSpec((None, t_new, B_WIDTH), lambda b, c: (b, 0, 1)),
                  pl.BlockSpec((rows, kc), lambda b, c: (0, c)),
                  pl.BlockSpec((rows, t_new), lambda b, c: (0, 0)),
                  pl.BlockSpec((rows, B_WIDTH), lambda b, c: (0, 0))],
        out_specs=pl.BlockSpec((None, t_new, B_WIDTH), lambda b, c: (b, 0, 0)),
        out_shape=jax.ShapeDtypeStruct((bsz, t_new, B_WIDTH), F32),
        scratch_shapes=[pltpu.VMEM((rows, 1), F32), pltpu.VMEM((rows, 1), F32), pltpu.VMEM((rows, B_WIDTH), F32)],
        compiler_params=_cp("parallel", "arbitrary"),
        name="attn_sample",
    )(qbd, cache, cache, kv_new, kv_new, tabc, tabn, jnp.asarray(mask))


def _outproj_even_kernel(*refs):
    ya_ref, yb_ref, x_ref, w_ref, g_ref, b_ref = refs[:6]
    o_ref = refs[-1]
    y = jnp.dot(ya_ref[...].astype(BF16), w_ref[0:RG_WIDTH, :], preferred_element_type=F32)
    y = y + jnp.dot(yb_ref[...].astype(BF16), w_ref[RG_WIDTH:, :], preferred_element_type=F32)
    o_ref[...] = _ln_rows(ALPHA * x_ref[...] + y, g_ref[...], b_ref[...])


def _token_call(kernel_fn, name, n, x_off_rows, out_rows, out_off_rows, row_inputs, x_arr, fixed_inputs, prev):
    tm = _row_tile(n)
    nt = n // tm
    xo, oo = x_off_rows // tm, out_off_rows // tm
    tail = prev is None and out_rows > out_off_rows + n
    if tail:
        assert out_off_rows == 0 and out_rows - n <= tm
        body = kernel_fn

        def kernel_fn(*refs):
            step = pl.program_id(0)
            pl.when(step < nt)(lambda: body(*refs))

            @pl.when(step >= nt)
            def _():
                refs[-1][...] = jnp.zeros(refs[-1].shape, F32)

    src = lambda i: jnp.minimum(i, nt - 1)
    in_specs = [pl.BlockSpec((tm, a.shape[1]), lambda i: (src(i), 0)) for a in row_inputs]
    in_specs.append(pl.BlockSpec((tm, D_MODEL), lambda i: (src(i) + xo, 0)))
    in_specs += [pl.BlockSpec(a.shape, lambda i: (0, 0)) for a in fixed_inputs]
    args = list(row_inputs) + [x_arr] + list(fixed_inputs)
    aliases = {}
    if prev is not None:
        in_specs.append(pl.BlockSpec(memory_space=pl.ANY))
        aliases = {len(args): 0}
        args.append(prev)
    return pl.pallas_call(
        kernel_fn,
        grid=(nt + int(tail),),
        in_specs=in_specs,
        out_specs=pl.BlockSpec((tm, D_MODEL), lambda i: (i + oo, 0)),
        out_shape=jax.ShapeDtypeStruct((out_rows, D_MODEL), F32),
        input_output_aliases=aliases,
        compiler_params=_cp("parallel"),
        name=name,
    )(*args)


def _inproj_odd_kernel(x_ref, w_ref, wdt_ref, z_ref, xbc_ref, dt_ref):
    xb = x_ref[...].astype(BF16)
    z_ref[...] = jnp.dot(xb, w_ref[:, 0:SSD_D_INNER], preferred_element_type=F32)
    xbc_ref[...] = jnp.dot(xb, w_ref[:, SSD_D_INNER:SSD_MAIN], preferred_element_type=F32)
    dt_ref[...] = jnp.dot(xb, wdt_ref[...], preferred_element_type=F32)


def _inproj_odd(x_all, row0, n, w, wdt):
    tm = _row_tile(n)
    off = row0 // tm
    widths = (SSD_D_INNER, SSD_CONV_DIM, LANES)
    return pl.pallas_call(
        _inproj_odd_kernel,
        grid=(n // tm,),
        in_specs=[pl.BlockSpec((tm, D_MODEL), lambda i: (i + off, 0)),
                  pl.BlockSpec((D_MODEL, SSD_MAIN), lambda i: (0, 0)),
                  pl.BlockSpec((D_MODEL, LANES), lambda i: (0, 0))],
        out_specs=[pl.BlockSpec((tm, c), lambda i: (i, 0)) for c in widths],
        out_shape=[jax.ShapeDtypeStruct((n, c), F32) for c in widths],
        compiler_params=_cp("parallel"),
        name="inproj_odd",
    )(x_all, w, wdt)


def _ssd_kernel(xbc_ref, dt_ref, cbuf_ref, h0_ref, cw_ref, cb_ref, dtb_ref, a_ref, dsk_ref,
                y_ref, cnew_ref, hT_ref, xx, dq, S, *, tt):
    c = pl.program_id(1)
    q = SSD_Q
    hist = CONV_W - 1
    base = 8

    @pl.when(c == 0)
    def _():
        if tt < q:
            xx[...] = jnp.zeros(xx.shape, F32)
            dq[...] = jnp.zeros(dq.shape, F32)
        xx[base - hist:base, :] = cbuf_ref[...]
        S[...] = h0_ref[...]

    @pl.when(c > 0)
    def _():
        xx[base - hist:base, :] = xx[base + tt - hist:base + tt, :]

    xx[base:base + tt, :] = xbc_ref[...]
    dq[0:tt, :] = dt_ref[...]

    @pl.when(c == pl.num_programs(1) - 1)
    def _():
        cnew_ref[...] = xx[base + tt - hist:base + tt, :]

    def conv_cols(lo, hi):
        acc = cb_ref[:, lo:hi]
        for k in range(CONV_W):
            acc = acc + cw_ref[k:k + 1, lo:hi] * xx[base - hist + k:base - hist + k + q, lo:hi]
        return _silu(acc)

    row = lax.broadcasted_iota(I32, (q, q), 0)
    col = lax.broadcasted_iota(I32, (q, q), 1)
    causal = row >= col
    dt = _softplus(dq[...] + dtb_ref[...])
    if tt < q:
        dt = jnp.where(lax.broadcasted_iota(I32, (q, LANES), 0) < tt, dt, 0.0)
    da = dt * a_ref[...]
    cum = jnp.dot(causal.astype(F32), da, precision=lax.Precision.HIGHEST, preferred_element_type=F32)
    cum_t = cum.T
    dt_t = dt.T
    cum_last = cum[q - 1:q, :]
    ecum = jnp.exp(cum)
    dtde = dt * jnp.exp(cum_last - cum)
    cd = jnp.exp(cum_last)
    lo = lax.broadcasted_iota(I32, (q, LANES), 1) < SSD_P
    lo1 = lo[0:1, :]
    nt = (((1,), (1,)), ((), ()))
    pairs_per_group = SSD_HEADS // SSD_GROUPS // 2

    for g in range(SSD_GROUPS):
        bm = conv_cols(SSD_D_INNER + g * SSD_N, SSD_D_INNER + (g + 1) * SSD_N)
        cm = conv_cols(SSD_D_INNER + SSD_GN + g * SSD_N, SSD_D_INNER + SSD_GN + (g + 1) * SSD_N)
        bmb, cmb = bm.astype(BF16), cm.astype(BF16)
        cb = lax.dot_general(cmb, bmb, nt, preferred_element_type=F32)
        bm_t = bm.T.astype(BF16)
        for pp in range(pairs_per_group):
            p = g * pairs_per_group + pp
            h0, h1 = 2 * p, 2 * p + 1
            cs = slice(p * LANES, (p + 1) * LANES)
            xs = conv_cols(p * LANES, (p + 1) * LANES)
            xsb = xs.astype(BF16)
            yd = []
            for h in (h0, h1):
                seg = cum[:, h:h + 1] - cum_t[h:h + 1, :]
                m = cb * jnp.where(causal, jnp.exp(jnp.where(causal, seg, 0.0)), 0.0) * dt_t[h:h + 1, :]
                yd.append(jnp.dot(m.astype(BF16), xsb, preferred_element_type=F32))
            s_old = S[:, cs]
            yo = jnp.dot(cmb, s_old.astype(BF16), preferred_element_type=F32)
            yo = yo * jnp.where(lo, ecum[:, h0:h0 + 1], ecum[:, h1:h1 + 1])
            y = jnp.where(lo, yd[0], yd[1]) + yo + dsk_ref[:, cs] * xs
            y_ref[:, cs] = y[0:tt, :]
            wts = jnp.where(lo, dtde[:, h0:h0 + 1], dtde[:, h1:h1 + 1])
            upd = jnp.dot(bm_t, (xs * wts).astype(BF16), preferred_element_type=F32)
            S[:, cs] = s_old * jnp.where(lo1, cd[:, h0:h0 + 1], cd[:, h1:h1 + 1]) + upd

    @pl.when(c == pl.num_programs(1) - 1)
    def _():
        hT_ref[...] = S[...]


def _ssd(xbc, dt, cbuf, h0t, cw, cb, dtb, a, dsk):
    bsz, t, _ = xbc.shape
    tt = min(t, SSD_Q)
    row = lambda b, c: (b, c, 0)
    per_b = lambda b, c: (b, 0, 0)
    fixed = lambda b, c: (0, 0)
    return pl.pallas_call(
        functools.partial(_ssd_kernel, tt=tt),
        grid=(bsz, t // tt),
        in_specs=[pl.BlockSpec((None, tt, SSD_CONV_DIM), row), pl.BlockSpec((None, tt, LANES), row),
                  pl.BlockSpec((None, CONV_W - 1, SSD_CONV_DIM), per_b),
                  pl.BlockSpec((None, SSD_N, SSD_D_INNER), per_b),
                  pl.BlockSpec((CONV_W, SSD_CONV_DIM), fixed), pl.BlockSpec((1, SSD_CONV_DIM), fixed),
                  pl.BlockSpec((1, LANES), fixed), pl.BlockSpec((1, LANES), fixed),
                  pl.BlockSpec((1, SSD_D_INNER), fixed)],
        out_specs=[pl.BlockSpec((None, tt, SSD_D_INNER), row),
                   pl.BlockSpec((None, CONV_W - 1, SSD_CONV_DIM), per_b),
                   pl.BlockSpec((None, SSD_N, SSD_D_INNER), per_b)],
        out_shape=[jax.ShapeDtypeStruct((bsz, t, SSD_D_INNER), F32),
                   jax.ShapeDtypeStruct((bsz, CONV_W - 1, SSD_CONV_DIM), F32),
                   jax.ShapeDtypeStruct((bsz, SSD_N, SSD_D_INNER), F32)],
        scratch_shapes=[pltpu.VMEM((8 + SSD_Q, SSD_CONV_DIM), F32), pltpu.VMEM((SSD_Q, LANES), F32),
                        pltpu.VMEM((SSD_N, SSD_D_INNER), F32)],
        compiler_params=_cp("parallel", "arbitrary"),
        name="ssd",
    )(xbc, dt, cbuf, h0t, cw, cb, dtb, a, dsk)


def _outproj_odd_kernel(*refs):
    y_ref, z_ref, x_ref, nw_ref, w_ref, g_ref, b_ref = refs[:7]
    o_ref = refs[-1]
    gw = SSD_D_INNER // SSD_GROUPS
    acc = None
    for g in range(SSD_GROUPS):
        cs = slice(g * gw, (g + 1) * gw)
        v = y_ref[:, cs] * _silu(z_ref[:, cs])
        v = v * lax.rsqrt(jnp.mean(v * v, -1, keepdims=True) + RMS_EPS) * nw_ref[:, cs]
        part = jnp.dot(v.astype(BF16), w_ref[cs, :], preferred_element_type=F32)
        acc = part if acc is None else acc + part
    o_ref[...] = _ln_rows(ALPHA * x_ref[...] + acc, g_ref[...], b_ref[...])


def _router_kernel(x_ref, w_ref, b_ref, idx_ref, gate_ref, rank_ref, cnt_ref, carry, *, tm):
    i = pl.program_id(0)

    @pl.when(i == 0)
    def _():
        carry[...] = jnp.zeros(carry.shape, F32)

    logits = jnp.dot(x_ref[...], w_ref[...], precision=lax.Precision.HIGHEST, preferred_element_type=F32) + b_ref[...]
    lane = lax.broadcasted_iota(I32, (tm, N_EXPERTS), 1)
    work = logits
    vals, hots = [], []
    for _ in range(TOP_K):
        m = jnp.max(work, axis=1, keepdims=True)
        ik = jnp.min(jnp.where(work == m, lane, N_EXPERTS), axis=1, keepdims=True)
        hot = lane == ik
        vals.append(m)
        hots.append(hot)
        work = jnp.where(hot, NEG_INF, work)
    es = [jnp.exp(v - vals[0]) for v in vals]
    den = es[0] + es[1] + es[2] + es[3]
    hot_all = (hots[0] | hots[1] | hots[2] | hots[3])
    r_i = lax.broadcasted_iota(I32, (tm, tm), 0)
    c_i = lax.broadcasted_iota(I32, (tm, tm), 1)
    before = (r_i > c_i).astype(BF16)
    pre = jnp.dot(before, hot_all.astype(BF16), preferred_element_type=F32) + carry[...]
    lane4 = lax.broadcasted_iota(I32, (tm, TOP_K), 1)
    idx = jnp.zeros((tm, TOP_K), I32)
    gate = jnp.zeros((tm, TOP_K), F32)
    rank = jnp.zeros((tm, TOP_K), I32)
    for k in range(TOP_K):
        ik = jnp.sum(jnp.where(hots[k], lane, 0), axis=1, keepdims=True)
        rk = jnp.sum(jnp.where(hots[k], pre, 0.0), axis=1, keepdims=True).astype(I32)
        idx = jnp.where(lane4 == k, ik, idx)
        rank = jnp.where(lane4 == k, rk, rank)
        gate = jnp.where(lane4 == k, es[k] / den, gate)
    idx_ref[...] = idx
    gate_ref[...] = gate
    rank_ref[...] = rank
    carry[...] = carry[...] + jnp.sum(hot_all.astype(F32), axis=0, keepdims=True)
    cnt_ref[...] = carry[...]


def _router(x_all, w, b):
    n = x_all.shape[0]
    tm = 384 if n % 384 == 0 else 128
    out4 = lambda dt: jax.ShapeDtypeStruct((n, TOP_K), dt)
    return pl.pallas_call(
        functools.partial(_router_kernel, tm=tm),
        grid=(n // tm,),
        in_specs=[pl.BlockSpec((tm, D_MODEL), lambda i: (i, 0)),
                  pl.BlockSpec((D_MODEL, N_EXPERTS), lambda i: (0, 0)),
                  pl.BlockSpec((1, N_EXPERTS), lambda i: (0, 0))],
        out_specs=[pl.BlockSpec((tm, TOP_K), lambda i: (i, 0))] * 3 + [pl.BlockSpec((1, N_EXPERTS), lambda i: (0, 0))],
        out_shape=[out4(I32), out4(F32), out4(I32), jax.ShapeDtypeStruct((1, N_EXPERTS), F32)],
        scratch_shapes=[pltpu.VMEM((1, N_EXPERTS), F32)],
        compiler_params=_cp("arbitrary"),
        name="router",
    )(x_all, w, b)


def _row_copy(src_hbm, row, dst, dst_row, sem):
    return pltpu.make_async_copy(src_hbm.at[pl.ds(row, 1), :], dst.at[pl.ds(dst_row, 1), :], sem)


def _ffn_kernel(te_ref, tos_ref, nv_ref, x_hbm, w1g_ref, w1l_ref, b1g_ref, b1l_ref, w2_ref, b2_ref, o_ref,
                xbuf, sem, *, rows):
    i = pl.program_id(0)
    nvalid = nv_ref[0]
    slot = i % 2

    def gather(tile, s):
        def issue(r, _):
            _row_copy(x_hbm, tos_ref[tile * rows + r], xbuf.at[s], r, sem.at[s]).start()
            return 0
        lax.fori_loop(0, rows, issue, 0, unroll=8)

    @pl.when(i == 0)
    def _():
        gather(0, 0)

    @pl.when(i + 1 < nvalid)
    def _():
        gather(i + 1, 1 - slot)

    @pl.when(i < nvalid)
    def _():
        pltpu.make_async_copy(x_hbm.at[pl.ds(0, rows), :], xbuf.at[slot], sem.at[slot]).wait()
        xb = xbuf[slot].astype(BF16)
        hg = jnp.dot(xb, w1g_ref[...], preferred_element_type=F32) + b1g_ref[...]
        hl = jnp.dot(xb, w1l_ref[...], preferred_element_type=F32) + b1l_ref[...]
        glu = jnp.minimum(hg, SWIGLU_LIMIT)
        lin = jnp.clip(hl, -SWIGLU_LIMIT, SWIGLU_LIMIT)
        act = glu * _sigmoid(SWIGLU_ALPHA * glu) * (lin + 1.0)
        o_ref[...] = jnp.dot(act.astype(BF16), w2_ref[...], preferred_element_type=F32) + b2_ref[...]

    @pl.when(i >= nvalid)
    def _():
        o_ref[...] = jnp.zeros(o_ref.shape, F32)


def _ffn(x_all, tile_expert, tok_of_slot, nvalid, w1g, w1l, b1g, b1l, w2, b2, ntiles):
    rows = MOE_ROWS
    wspec = lambda shape: pl.BlockSpec((None,) + shape, lambda i, te, tos, nv: (te[i], 0, 0))
    grid_spec = pltpu.PrefetchScalarGridSpec(
        num_scalar_prefetch=3,
        grid=(ntiles,),
        in_specs=[pl.BlockSpec(memory_space=pl.ANY),
                  wspec((D_MODEL, D_FF)), wspec((D_MODEL, D_FF)), wspec((1, D_FF)), wspec((1, D_FF)),
                  wspec((D_FF, D_MODEL)), wspec((1, D_MODEL))],
        out_specs=pl.BlockSpec((rows, D_MODEL), lambda i, te, tos, nv: (i, 0)),
        scratch_shapes=[pltpu.VMEM((2, rows, D_MODEL), F32), pltpu.SemaphoreType.DMA((2,))],
    )
    return pl.pallas_call(
        functools.partial(_ffn_kernel, rows=rows),
        grid_spec=grid_spec,
        out_shape=jax.ShapeDtypeStruct((ntiles * rows, D_MODEL), F32),
        compiler_params=_cp("arbitrary"),
        name="moe_ffn",
    )(tile_expert, tok_of_slot, nvalid, x_all, w1g, w1l, b1g, b1l, w2, b2)


def _combine_kernel(pos_ref, o_hbm, gate_ref, x_ref, g_ref, b_ref, out_ref, buf, sem, *, tm):
    i = pl.program_id(0)
    n = pl.num_programs(0)
    slot = i % 2

    def gather(tile, s):
        def issue(t, _):
            for k in range(TOP_K):
                _row_copy(o_hbm, pos_ref[(tile * tm + t) * TOP_K + k], buf.at[s, k], t, sem.at[s]).start()
            return 0
        lax.fori_loop(0, tm, issue, 0, unroll=4)

    @pl.when(i == 0)
    def _():
        gather(0, 0)

    @pl.when(i + 1 < n)
    def _():
        gather(i + 1, 1 - slot)

    f = None
    for k in range(TOP_K):
        pltpu.make_async_copy(o_hbm.at[pl.ds(0, tm), :], buf.at[slot, k], sem.at[slot]).wait()
    for k in range(TOP_K):
        part = gate_ref[:, k:k + 1] * buf[slot, k]
        f = part if f is None else f + part
    out_ref[...] = _ln_rows(ALPHA * x_ref[...] + f, g_ref[...], b_ref[...])


def _combine(pos_flat, o_sorted, gate, x_all, g, b):
    n = x_all.shape[0]
    tm = 128
    grid_spec = pltpu.PrefetchScalarGridSpec(
        num_scalar_prefetch=1,
        grid=(n // tm,),
        in_specs=[pl.BlockSpec(memory_space=pl.ANY),
                  pl.BlockSpec((tm, TOP_K), lambda i, pos: (i, 0)),
                  pl.BlockSpec((tm, D_MODEL), lambda i, pos: (i, 0)),
                  pl.BlockSpec((1, D_MODEL), lambda i, pos: (0, 0)),
                  pl.BlockSpec((1, D_MODEL), lambda i, pos: (0, 0))],
        out_specs=pl.BlockSpec((tm, D_MODEL), lambda i, pos: (i, 0)),
        scratch_shapes=[pltpu.VMEM((2, TOP_K, tm, D_MODEL), F32), pltpu.SemaphoreType.DMA((2,))],
    )
    return pl.pallas_call(
        functools.partial(_combine_kernel, tm=tm),
        grid_spec=grid_spec,
        out_shape=jax.ShapeDtypeStruct((n, D_MODEL), F32),
        compiler_params=_cp("arbitrary"),
        name="moe_combine",
    )(pos_flat, o_sorted, gate, x_all, g, b)


def _moe(x_all, w_r, b_r, w1, b1, w2, b2, g, b):
    n = x_all.shape[0]
    rows = MOE_ROWS
    ntiles = -(-(n * TOP_K + N_EXPERTS * (rows - 1)) // rows)
    idx, gate, rank, cnt = _router(x_all, w_r, b_r.reshape(1, N_EXPERTS))
    counts = cnt[0].astype(I32)
    padded = ((counts + rows - 1) // rows) * rows
    ends = jnp.cumsum(padded)
    starts = ends - padded
    pos = starts[idx] + rank
    tile_expert = jnp.minimum(jnp.searchsorted(ends, jnp.arange(ntiles, dtype=I32) * rows, side="right"),
                              N_EXPERTS - 1).astype(I32)
    nvalid = (ends[-1:] // rows).astype(I32)
    tok = jnp.broadcast_to(jnp.arange(n, dtype=I32)[:, None], (n, TOP_K))
    tok_of_slot = jnp.zeros((ntiles * rows,), I32).at[pos.reshape(-1)].set(tok.reshape(-1))
    w1g = w1[:, :, 0::2].astype(BF16)
    w1l = w1[:, :, 1::2].astype(BF16)
    b1g = b1[:, None, 0::2]
    b1l = b1[:, None, 1::2]
    o_sorted = _ffn(x_all, tile_expert, tok_of_slot, nvalid, w1g, w1l, b1g, b1l,
                    w2.astype(BF16), b2[:, None, :], ntiles)
    return _combine(pos.reshape(-1), o_sorted, gate, x_all, g.reshape(1, -1), b.reshape(1, -1))


def _block_diag(w):
    nb, bw, _ = w.shape
    eye = jnp.eye(nb, dtype=w.dtype)
    return (eye[:, None, :, None] * w[:, :, None, :]).reshape(nb * bw, nb * bw)


def _mix_even(x_arr, out_row0, n_total, bsz, t, rg_conv, rg_h, cache, p, prev):
    n = bsz * t
    xa, ga, q, kv = _inproj_even(x_arr, 0, n, p["w_in"])
    ya, conv_new, h_last = _rglru(
        xa.reshape(bsz, t, RG_WIDTH), ga.reshape(bsz, t, RG_WIDTH), rg_conv, rg_h.reshape(bsz, 1, RG_WIDTH),
        p["rg_conv_w"], p["rg_conv_b"], p["rg_wg"], p["rg_bg"], p["rg_c"])
    q3 = q.reshape(bsz, t, B_WIDTH)
    kv3 = kv.reshape(bsz, t, 2 * B_WIDTH)
    if cache is None:
        yb = _attn_prompt(q3, kv3, p["rel_bias"])
    else:
        yb = _attn_sample(q3, cache, kv3, p["rel_bias"])
    x1 = _token_call(_outproj_even_kernel, "outproj_even", n, 0, n_total, out_row0,
                     [ya.reshape(n, RG_WIDTH), yb.reshape(n, B_WIDTH)], x_arr,
                     [p["w_out"], p["ln_g0"], p["ln_b0"]], prev)
    return x1, conv_new, h_last.reshape(bsz, RG_WIDTH), kv3


def _mix_odd(x_all, row0, bsz, t, ssd_conv, ssd_h, p, prev):
    n = bsz * t
    z, xbc, dt = _inproj_odd(x_all, row0, n, p["ssd_w_main"], p["ssd_w_dt"])
    h0t = jnp.transpose(ssd_h.astype(F32), (0, 3, 1, 2)).reshape(bsz, SSD_N, SSD_D_INNER)
    y, conv_new, h_t = _ssd(xbc.reshape(bsz, t, SSD_CONV_DIM), dt.reshape(bsz, t, LANES), ssd_conv, h0t,
                            p["ssd_conv_w"], p["ssd_conv_b"], p["ssd_dtb"], p["ssd_a"], p["ssd_dsk"])
    h_last = jnp.transpose(h_t.reshape(bsz, SSD_N, SSD_HEADS, SSD_P), (0, 2, 3, 1))
    x1 = _token_call(_outproj_odd_kernel, "outproj_odd", n, row0, x_all.shape[0], row0,
                     [y.reshape(n, SSD_D_INNER), z], x_all,
                     [p["ssd_norm_w"], p["ssd_w_out"], p["ln_g2"], p["ln_b2"]], prev)
    return x1, conv_new, h_last


def kernel(x_prompt, x_sample, state_rglru_conv, state_rglru_h, cache_swa_kv, state_ssd_conv, state_ssd_h,
           rel_bias, w_in_mix, rg_conv_w, rg_conv_b, rg_w_a, rg_b_a, rg_w_i, rg_b_i, rg_lambda, w_out_mix,
           ssd_w_in, ssd_conv_w, ssd_conv_b, ssd_dt_bias, ssd_a_log, ssd_d, ssd_norm_w, ssd_w_out,
           ln_g, ln_b, router_w, router_b, exp_w1, exp_b1, exp_w2, exp_b2):
    bp, tp, _ = x_prompt.shape
    bs, ts, _ = x_sample.shape
    n_p, n_s = bp * tp, bs * ts
    n_total = n_p + n_s
    row2 = lambda v: v.reshape(1, -1).astype(F32)
    pad_heads = lambda v: jnp.pad(v.astype(F32), (0, LANES - SSD_HEADS)).reshape(1, LANES)

    p = dict(
        rel_bias=rel_bias,
        w_in=w_in_mix[0].astype(BF16),
        rg_conv_w=rg_conv_w[0].astype(F32), rg_conv_b=row2(rg_conv_b[0]),
        rg_wg=jnp.concatenate([_block_diag(rg_w_a[0]), _block_diag(rg_w_i[0])], axis=1).astype(BF16),
        rg_bg=jnp.concatenate([rg_b_a[0].reshape(1, -1), rg_b_i[0].reshape(1, -1)], axis=1).astype(F32),
        rg_c=row2(-RG_C * jax.nn.softplus(-rg_lambda[0].astype(F32))),
        w_out=w_out_mix[0].astype(BF16),
        ln_g0=row2(ln_g[0, 0]), ln_b0=row2(ln_b[0, 0]),
        ssd_w_main=ssd_w_in[0][:, :SSD_MAIN].astype(BF16),
        ssd_w_dt=jnp.pad(ssd_w_in[0][:, SSD_MAIN:], ((0, 0), (0, LANES - SSD_HEADS))).astype(BF16),
        ssd_conv_w=ssd_conv_w[0].astype(F32), ssd_conv_b=row2(ssd_conv_b[0]),
        ssd_dtb=pad_heads(ssd_dt_bias[0]), ssd_a=pad_heads(-jnp.exp(ssd_a_log[0].astype(F32))),
        ssd_dsk=row2(jnp.repeat(ssd_d[0].astype(F32), SSD_P)),
        ssd_norm_w=row2(ssd_norm_w[0]), ssd_w_out=ssd_w_out[0].astype(BF16),
        ln_g2=row2(ln_g[1, 0]), ln_b2=row2(ln_b[1, 0]),
    )

    xp = x_prompt.reshape(n_p, D_MODEL)
    xs = x_sample.reshape(n_s, D_MODEL)
    zeros = lambda *s: jnp.zeros(s, F32)
    x1, p_rg_conv, p_rg_h, p_kv = _mix_even(xp, 0, n_total, bp, tp, zeros(bp, CONV_W - 1, RG_WIDTH),
                                            zeros(bp, RG_WIDTH), None, p, None)
    cache = cache_swa_kv[0].reshape(bs, cache_swa_kv.shape[2], 2 * B_WIDTH)
    x1, s_rg_conv, s_rg_h, s_kv = _mix_even(xs, n_p, n_total, bs, ts, state_rglru_conv[0], state_rglru_h[0],
                                            cache, p, x1)
    x2 = _moe(x1, router_w[0].astype(F32), router_b[0].astype(F32), exp_w1[0], exp_b1[0], exp_w2[0], exp_b2[0],
              ln_g[0, 1], ln_b[0, 1])

    x3, p_ssd_conv, p_ssd_h = _mix_odd(x2, 0, bp, tp, zeros(bp, CONV_W - 1, SSD_CONV_DIM),
                                       zeros(bp, SSD_HEADS, SSD_P, SSD_N), p, None)
    x3, s_ssd_conv, s_ssd_h = _mix_odd(x2, n_p, bs, ts, state_ssd_conv[0], state_ssd_h[0], p, x3)
    x4 = _moe(x3, router_w[1].astype(F32), router_b[1].astype(F32), exp_w1[1], exp_b1[1], exp_w2[1], exp_b2[1],
              ln_g[1, 1], ln_b[1, 1])

    kv_shape = lambda b, t: (1, b, t, 2, B_HEADS, B_HD)
    return (x4[:n_p].reshape(bp, tp, D_MODEL), x4[n_p:].reshape(bs, ts, D_MODEL),
            p_rg_conv[None], p_rg_h[None], p_kv[:, -min(MAX_WINDOW, tp):].reshape(kv_shape(bp, min(MAX_WINDOW, tp))),
            p_ssd_conv[None], p_ssd_h[None],
            s_rg_conv[None], s_rg_h[None], s_kv.reshape(kv_shape(bs, ts)),
            s_ssd_conv[None], s_ssd_h[None])
```

```python
import functools
import math

import numpy as np
import jax
import jax.numpy as jnp
from jax import lax
from jax.experimental import pallas as pl
from jax.experimental.pallas import tpu as pltpu

F32 = jnp.float32
BF16 = jnp.bfloat16
I32 = jnp.int32

D_MODEL = 1024
DEPTH = 2
ALPHA = (2 * DEPTH) ** 0.25
LN_EPS = 1e-5
RMS_EPS = 1e-5
CONV_W = 4
RG_BLOCKS = 12
RG_BW = 64
RG_WIDTH = RG_BLOCKS * RG_BW
RG_C = 8.0
B_HEADS = 12
B_HD = 64
B_WIDTH = B_HEADS * B_HD
PATTERNS = ((128, 1), (512, 4), (2048, 16))
MAX_WINDOW = 2048
N_BUCKETS = 32
MAX_DISTANCE = MAX_WINDOW
EVEN_IN = 2 * RG_WIDTH + 3 * B_WIDTH
SSD_D_INNER = 2 * D_MODEL
SSD_P = 64
SSD_HEADS = SSD_D_INNER // SSD_P
SSD_N = 128
SSD_GROUPS = 4
SSD_GN = SSD_GROUPS * SSD_N
SSD_CONV_DIM = SSD_D_INNER + 2 * SSD_GN
SSD_MAIN = SSD_D_INNER + SSD_CONV_DIM
N_EXPERTS = 32
TOP_K = 4
D_FF = D_MODEL
SWIGLU_LIMIT = 7.0
SWIGLU_ALPHA = 1.702

LANES = 128
VMEM_LIMIT = 56 * 1024 * 1024

SSD_Q = 128
ATT_BLK = 256
MOE_ROWS = 256
NEG_INF = float("-inf")


def _cp(*sem):
    return pltpu.CompilerParams(dimension_semantics=sem, vmem_limit_bytes=VMEM_LIMIT)


def _ln_rows(v, g, b):
    mu = jnp.mean(v, -1, keepdims=True)
    xc = v - mu
    var = jnp.mean(xc * xc, -1, keepdims=True)
    return xc * lax.rsqrt(var + LN_EPS) * g + b


def _sigmoid(x):
    return 1.0 / (1.0 + jnp.exp(-x))


def _silu(x):
    return x * _sigmoid(x)


def _softplus(x):
    return jnp.maximum(x, 0.0) + jnp.log1p(jnp.exp(-jnp.abs(x)))


def _gelu_tanh(x):
    return 0.5 * x * (1.0 + jnp.tanh(math.sqrt(2.0 / math.pi) * (x + 0.044715 * (x * x * x))))


def _row_tile(n):
    return min(n, 256)


def _inproj_even_kernel(x_ref, w_ref, xa_ref, ga_ref, q_ref, kv_ref):
    xb = x_ref[...].astype(BF16)
    c0, c1, c2 = RG_WIDTH, 2 * RG_WIDTH, 2 * RG_WIDTH + B_WIDTH
    xa_ref[...] = jnp.dot(xb, w_ref[:, 0:c0], preferred_element_type=F32)
    ga_ref[...] = jnp.dot(xb, w_ref[:, c0:c1], preferred_element_type=F32)
    q_ref[...] = jnp.dot(xb, w_ref[:, c1:c2], preferred_element_type=F32)
    kv_ref[...] = jnp.dot(xb, w_ref[:, c2:EVEN_IN], preferred_element_type=F32)


def _inproj_even(x_all, row0, n, w):
    tm = _row_tile(n)
    off = row0 // tm
    widths = (RG_WIDTH, RG_WIDTH, B_WIDTH, 2 * B_WIDTH)
    return pl.pallas_call(
        _inproj_even_kernel,
        grid=(n // tm,),
        in_specs=[pl.BlockSpec((tm, D_MODEL), lambda i: (i + off, 0)),
                  pl.BlockSpec((D_MODEL, EVEN_IN), lambda i: (0, 0))],
        out_specs=[pl.BlockSpec((tm, c), lambda i: (i, 0)) for c in widths],
        out_shape=[jax.ShapeDtypeStruct((n, c), F32) for c in widths],
        compiler_params=_cp("parallel"),
        name="inproj_even",
    )(x_all, w)


def _rglru_kernel(xa_ref, ga_ref, cbuf_ref, h0_ref, cw_ref, cb_ref, wg_ref, bg_ref, c_ref,
                  ya_ref, cnew_ref, hlast_ref, xx, a_s, u_s, hc, *, tt):
    j = pl.program_id(1)
    hist = CONV_W - 1
    base = 8

    @pl.when(j == 0)
    def _():
        xx[base - hist:base, :] = cbuf_ref[...]
        hc[...] = h0_ref[...]

    @pl.when(j > 0)
    def _():
        xx[base - hist:base, :] = xx[base + tt - hist:base + tt, :]

    xx[base:base + tt, :] = xa_ref[...]
    conv = cb_ref[...]
    for k in range(CONV_W):
        conv = conv + cw_ref[k:k + 1, :] * xx[base - hist + k:base - hist + k + tt, :]
    gates = jnp.dot(conv.astype(BF16), wg_ref[...], preferred_element_type=F32) + bg_ref[...]
    r = _sigmoid(gates[:, :RG_WIDTH])
    ig = _sigmoid(gates[:, RG_WIDTH:])
    log_a = c_ref[...] * r
    a = jnp.exp(log_a)
    a_s[...] = a
    u_s[...] = jnp.sqrt(-jnp.tanh(log_a) * (a * a + 1.0)) * (ig * conv)

    def step(t, h):
        h = a_s[pl.ds(t, 1), :] * h + u_s[pl.ds(t, 1), :]
        u_s[pl.ds(t, 1), :] = h
        return h

    h = lax.fori_loop(0, tt, step, hc[...], unroll=min(tt, 8))
    hc[...] = h
    ya_ref[...] = _gelu_tanh(ga_ref[...]) * u_s[...]

    @pl.when(j == pl.num_programs(1) - 1)
    def _():
        hlast_ref[...] = h
        cnew_ref[...] = xx[base + tt - hist:base + tt, :]


def _rglru(xa, ga, cbuf, h0, cw, cb, wg, bg, cvec):
    bsz, t, c = xa.shape
    tt = min(t, 256)
    row = lambda b, j: (b, j, 0)
    per_b = lambda b, j: (b, 0, 0)
    fixed = lambda b, j: (0, 0)
    return pl.pallas_call(
        functools.partial(_rglru_kernel, tt=tt),
        grid=(bsz, t // tt),
        in_specs=[pl.BlockSpec((None, tt, c), row), pl.BlockSpec((None, tt, c), row),
                  pl.BlockSpec((None, CONV_W - 1, c), per_b), pl.BlockSpec((None, 1, c), per_b),
                  pl.BlockSpec((CONV_W, c), fixed), pl.BlockSpec((1, c), fixed),
                  pl.BlockSpec((c, 2 * c), fixed), pl.BlockSpec((1, 2 * c), fixed),
                  pl.BlockSpec((1, c), fixed)],
        out_specs=[pl.BlockSpec((None, tt, c), row), pl.BlockSpec((None, CONV_W - 1, c), per_b),
                   pl.BlockSpec((None, 1, c), per_b)],
        out_shape=[jax.ShapeDtypeStruct((bsz, t, c), F32),
                   jax.ShapeDtypeStruct((bsz, CONV_W - 1, c), F32),
                   jax.ShapeDtypeStruct((bsz, 1, c), F32)],
        scratch_shapes=[pltpu.VMEM((8 + tt, c), F32), pltpu.VMEM((tt, c), F32),
                        pltpu.VMEM((tt, c), F32), pltpu.VMEM((1, c), F32)],
        compiler_params=_cp("parallel", "arbitrary"),
        name="rglru",
    )(xa, ga, cbuf, h0, cw, cb, wg, bg, cvec)


def _t5_bucket(dist):
    max_exact = N_BUCKETS // 2
    safe = np.maximum(dist, 1)
    large = max_exact + (np.log(safe / max_exact) / np.log(MAX_DISTANCE / max_exact)
                         * (N_BUCKETS - max_exact)).astype(np.int32)
    return np.where(dist < max_exact, dist, np.minimum(large, N_BUCKETS - 1)).astype(np.int32)


def _distance_bias(rel_bias, max_d):
    d = np.arange(max_d + 1)
    mult = np.zeros(max_d + 1, np.float64)
    for w, dil in PATTERNS:
        mult += ((d % dil == 0) & (d <= w)).astype(np.float64)
    logm = np.where(mult > 0, np.log(np.maximum(mult, 1.0)), -np.inf).astype(np.float32)
    vals = rel_bias.astype(F32)[_t5_bucket(d)].T + jnp.asarray(logm)[None, :]
    return jnp.concatenate([vals, jnp.full((vals.shape[0], 1), NEG_INF, F32)], axis=1)


def _attn_prompt_kernel(q_ref, k_ref, v_ref, rv_ref, o_ref, tab_ref, *, blk, nq):
    qi = pl.program_id(2)

    @pl.when((pl.program_id(1) == 0) & (qi == 0))
    def _():
        for h in range(2):
            for d in range(nq):
                win = rv_ref[h:h + 1, (nq - 1 - d) * blk:(nq + 1 - d) * blk]
                rolled = pltpu.roll(jnp.broadcast_to(win, (blk, 2 * blk)), 0, 1, stride=1, stride_axis=0)
                tab_ref[h, d] = rolled[:, blk:]

    lo = lax.broadcasted_iota(I32, (blk, LANES), 1) < B_HD
    q2 = q_ref[...] * (B_HD ** -0.5)
    qh = (jnp.where(lo, q2, 0.0).astype(BF16), jnp.where(lo, 0.0, q2).astype(BF16))
    nt = (((1,), (1,)), ((), ()))

    def body(d, carry):
        m0, l0, m1, l1, acc = carry
        start = pl.multiple_of((qi - d) * blk, blk)
        k = k_ref[pl.ds(start, blk), :].astype(BF16)
        v = v_ref[pl.ds(start, blk), :].astype(BF16)
        new = []
        for h, (m, l) in enumerate(((m0, l0), (m1, l1))):
            s = lax.dot_general(qh[h], k, nt, preferred_element_type=F32) + tab_ref[h, d]
            mn = jnp.maximum(m, jnp.max(s, axis=1, keepdims=True))
            p = jnp.exp(s - mn)
            al = jnp.exp(m - mn)
            ln = al * l + jnp.sum(p, axis=1, keepdims=True)
            pv = jnp.dot(p.astype(BF16), v, preferred_element_type=F32)
            new.append((mn, ln, al, pv))
        (m0, l0, a0, pv0), (m1, l1, a1, pv1) = new
        acc = jnp.where(lo, a0, a1) * acc + jnp.where(lo, pv0, pv1)
        return m0, l0, m1, l1, acc

    minf = jnp.full((blk, 1), NEG_INF, F32)
    zero = jnp.zeros((blk, 1), F32)
    m0, l0, m1, l1, acc = lax.fori_loop(0, qi + 1, body, (minf, zero, minf, zero, jnp.zeros((blk, LANES), F32)))
    o_ref[...] = acc / jnp.where(lo, l0, l1)


def _attn_prompt(q, kv, rel_bias):
    bsz, t, _ = q.shape
    blk = min(ATT_BLK, t)
    nq = t // blk
    npair = B_HEADS // 2
    vals = _distance_bias(rel_bias, t)[:, :t + 1]
    rv = jnp.concatenate([jnp.full((B_HEADS, blk - 1), NEG_INF, F32), vals], axis=1)[:, ::-1]
    rv = rv.reshape(npair, 2, t + blk)
    return pl.pallas_call(
        functools.partial(_attn_prompt_kernel, blk=blk, nq=nq),
        grid=(npair, bsz, nq),
        in_specs=[pl.BlockSpec((None, blk, LANES), lambda p, b, i: (b, i, p)),
                  pl.BlockSpec((None, t, LANES), lambda p, b, i: (b, 0, p)),
                  pl.BlockSpec((None, t, LANES), lambda p, b, i: (b, 0, npair + p)),
                  pl.BlockSpec((None, 2, t + blk), lambda p, b, i: (p, 0, 0))],
        out_specs=pl.BlockSpec((None, blk, LANES), lambda p, b, i: (b, i, p)),
        out_shape=jax.ShapeDtypeStruct((bsz, t, B_WIDTH), F32),
        scratch_shapes=[pltpu.VMEM((2, nq, blk, blk), F32)],
        compiler_params=_cp("arbitrary", "arbitrary", "arbitrary"),
        name="attn_prompt",
    )(q, kv, kv, rv)


ATT_S_ROWS = 16


def _attn_sample_kernel(qbd_ref, kc_ref, vc_ref, kn_ref, vn_ref, tabc_ref, tabn_ref, mask_ref, o_ref,
                        m_s, l_s, acc_s, *, t_new):
    c = pl.program_id(1)
    rows = t_new * ATT_S_ROWS
    nt = (((1,), (1,)), ((), ()))

    @pl.when(c == 0)
    def _():
        m_s[...] = jnp.full((rows, 1), NEG_INF, F32)
        l_s[...] = jnp.zeros((rows, 1), F32)
        acc_s[...] = jnp.zeros((rows, B_WIDTH), F32)

    qf = qbd_ref[...] * (B_HD ** -0.5)
    qb = qf.astype(BF16)
    s = lax.dot_general(qb, kc_ref[...].astype(BF16), nt, preferred_element_type=F32) + tabc_ref[...]
    m = m_s[...]
    mn = jnp.maximum(m, jnp.max(s, axis=1, keepdims=True))
    last = c == pl.num_programs(1) - 1

    sn = [jnp.sum(qf * kn_ref[j:j + 1, :], axis=1, keepdims=True) + tabn_ref[:, j:j + 1] for j in range(t_new)]
    sn = [jnp.where(last, x, NEG_INF) for x in sn]
    for x in sn:
        mn = jnp.maximum(mn, x)
    safe = jnp.where(mn == NEG_INF, 0.0, mn)
    p = jnp.exp(s - safe)
    al = jnp.exp(m - safe)
    l = al * l_s[...] + jnp.sum(p, axis=1, keepdims=True)
    acc = al * acc_s[...] + jnp.dot(p.astype(BF16), vc_ref[...].astype(BF16), preferred_element_type=F32)
    for j, x in enumerate(sn):
        pj = jnp.exp(x - safe)
        l = l + pj
        acc = acc + pj * vn_ref[j:j + 1, :]
    m_s[...] = mn
    l_s[...] = l
    acc_s[...] = acc

    @pl.when(last)
    def _():
        om = (acc / l) * mask_ref[...]
        for t in range(t_new):
            o_ref[t:t + 1, :] = jnp.sum(om[t * ATT_S_ROWS:(t + 1) * ATT_S_ROWS, :], axis=0, keepdims=True)


def _attn_sample(q, cache, kv_new, rel_bias):
    bsz, t_new, _ = q.shape
    w = cache.shape[1]
    rows = t_new * ATT_S_ROWS
    vals = _distance_bias(rel_bias, MAX_WINDOW)
    vals = jnp.concatenate([vals, jnp.zeros((ATT_S_ROWS - B_HEADS, vals.shape[1]), F32)], axis=0)
    tq = np.repeat(np.arange(t_new), ATT_S_ROWS)
    hq = np.tile(np.arange(ATT_S_ROWS), t_new)
    ext = jnp.concatenate([vals[:, :MAX_WINDOW + 1], jnp.full((ATT_S_ROWS, w + t_new - MAX_WINDOW), NEG_INF, F32)], axis=1)
    ext = ext.at[B_HEADS:, :].set(0.0)
    tabc = jnp.stack([ext[:, t + 1:w + t + 1][:, ::-1] for t in range(t_new)]).reshape(rows, w)
    dist_n = tq[:, None] - np.arange(t_new)[None, :]
    dist_n = np.where(dist_n < 0, MAX_WINDOW + 1, dist_n)
    tabn = vals[jnp.asarray(hq)[:, None], jnp.asarray(dist_n)]
    mask = (hq[:, None] == (np.arange(B_WIDTH)[None, :] // B_HD)).astype(np.float32)
    qbd = (q[:, :, None, :] * jnp.asarray(mask.reshape(t_new, ATT_S_ROWS, B_WIDTH))[None]).reshape(bsz, rows, B_WIDTH)
    kc = 1024
    return pl.pallas_call(
        functools.partial(_attn_sample_kernel, t_new=t_new),
        grid=(bsz, w // kc),
        in_specs=[pl.BlockSpec((None, rows, B_WIDTH), lambda b, c: (b, 0, 0)),
                  pl.BlockSpec((None, kc, B_WIDTH), lambda b, c: (b, c, 0)),
                  pl.BlockSpec((None, kc, B_WIDTH), lambda b, c: (b, c, 1)),
                  pl.BlockSpec((None, t_new, B_WIDTH), lambda b, c: (b, 0, 0)),
                  pl.BlockSpec((None, t_new, B_WIDTH), lambda b, c: (b, 0, 1)),
                  pl.BlockSpec((rows, kc), lambda b, c: (0, c)),
                  pl.BlockSpec((rows, t_new), lambda b, c: (0, 0)),
                  pl.BlockSpec((rows, B_WIDTH), lambda b, c: (0, 0))],
        out_specs=pl.BlockSpec((None, t_new, B_WIDTH), lambda b, c: (b, 0, 0)),
        out_shape=jax.ShapeDtypeStruct((bsz, t_new, B_WIDTH), F32),
        scratch_shapes=[pltpu.VMEM((rows, 1), F32), pltpu.VMEM((rows, 1), F32), pltpu.VMEM((rows, B_WIDTH), F32)],
        compiler_params=_cp("parallel", "arbitrary"),
        name="attn_sample",
    )(qbd, cache, cache, kv_new, kv_new, tabc, tabn, jnp.asarray(mask))


def _outproj_even_kernel(*refs):
    ya_ref, yb_ref, x_ref, w_ref, g_ref, b_ref = refs[:6]
    o_ref = refs[-1]
    y = jnp.dot(ya_ref[...].astype(BF16), w_ref[0:RG_WIDTH, :], preferred_element_type=F32)
    y = y + jnp.dot(yb_ref[...].astype(BF16), w_ref[RG_WIDTH:, :], preferred_element_type=F32)
    o_ref[...] = _ln_rows(ALPHA * x_ref[...] + y, g_ref[...], b_ref[...])


def _token_call(kernel_fn, name, n, x_off_rows, out_rows, out_off_rows, row_inputs, x_arr, fixed_inputs, prev):
    tm = _row_tile(n)
    nt = n // tm
    xo, oo = x_off_rows // tm, out_off_rows // tm
    tail = prev is None and out_rows > out_off_rows + n
    if tail:
        assert out_off_rows == 0 and out_rows - n <= tm
        body = kernel_fn

        def kernel_fn(*refs):
            step = pl.program_id(0)
            pl.when(step < nt)(lambda: body(*refs))

            @pl.when(step >= nt)
            def _():
                refs[-1][...] = jnp.zeros(refs[-1].shape, F32)

    src = lambda i: jnp.minimum(i, nt - 1)
    in_specs = [pl.BlockSpec((tm, a.shape[1]), lambda i: (src(i), 0)) for a in row_inputs]
    in_specs.append(pl.BlockSpec((tm, D_MODEL), lambda i: (src(i) + xo, 0)))
    in_specs += [pl.BlockSpec(a.shape, lambda i: (0, 0)) for a in fixed_inputs]
    args = list(row_inputs) + [x_arr] + list(fixed_inputs)
    aliases = {}
    if prev is not None:
        in_specs.append(pl.BlockSpec(memory_space=pl.ANY))
        aliases = {len(args): 0}
        args.append(prev)
    return pl.pallas_call(
        kernel_fn,
        grid=(nt + int(tail),),
        in_specs=in_specs,
        out_specs=pl.BlockSpec((tm, D_MODEL), lambda i: (i + oo, 0)),
        out_shape=jax.ShapeDtypeStruct((out_rows, D_MODEL), F32),
        input_output_aliases=aliases,
        compiler_params=_cp("parallel"),
        name=name,
    )(*args)


def _inproj_odd_kernel(x_ref, w_ref, wdt_ref, z_ref, xbc_ref, dt_ref):
    xb = x_ref[...].astype(BF16)
    z_ref[...] = jnp.dot(xb, w_ref[:, 0:SSD_D_INNER], preferred_element_type=F32)
    xbc_ref[...] = jnp.dot(xb, w_ref[:, SSD_D_INNER:SSD_MAIN], preferred_element_type=F32)
    dt_ref[...] = jnp.dot(xb, wdt_ref[...], preferred_element_type=F32)


def _inproj_odd(x_all, row0, n, w, wdt):
    tm = _row_tile(n)
    off = row0 // tm
    widths = (SSD_D_INNER, SSD_CONV_DIM, LANES)
    return pl.pallas_call(
        _inproj_odd_kernel,
        grid=(n // tm,),
        in_specs=[pl.BlockSpec((tm, D_MODEL), lambda i: (i + off, 0)),
                  pl.BlockSpec((D_MODEL, SSD_MAIN), lambda i: (0, 0)),
                  pl.BlockSpec((D_MODEL, LANES), lambda i: (0, 0))],
        out_specs=[pl.BlockSpec((tm, c), lambda i: (i, 0)) for c in widths],
        out_shape=[jax.ShapeDtypeStruct((n, c), F32) for c in widths],
        compiler_params=_cp("parallel"),
        name="inproj_odd",
    )(x_all, w, wdt)


def _ssd_kernel(xbc_ref, dt_ref, cbuf_ref, h0_ref, cw_ref, cb_ref, dtb_ref, a_ref, dsk_ref,
                y_ref, cnew_ref, hT_ref, xx, dq, S, *, tt):
    c = pl.program_id(1)
    q = SSD_Q
    hist = CONV_W - 1
    base = 8

    @pl.when(c == 0)
    def _():
        if tt < q:
            xx[...] = jnp.zeros(xx.shape, F32)
            dq[...] = jnp.zeros(dq.shape, F32)
        xx[base - hist:base, :] = cbuf_ref[...]
        S[...] = h0_ref[...]

    @pl.when(c > 0)
    def _():
        xx[base - hist:base, :] = xx[base + tt - hist:base + tt, :]

    xx[base:base + tt, :] = xbc_ref[...]
    dq[0:tt, :] = dt_ref[...]

    @pl.when(c == pl.num_programs(1) - 1)
    def _():
        cnew_ref[...] = xx[base + tt - hist:base + tt, :]

    def conv_cols(lo, hi):
        acc = cb_ref[:, lo:hi]
        for k in range(CONV_W):
            acc = acc + cw_ref[k:k + 1, lo:hi] * xx[base - hist + k:base - hist + k + q, lo:hi]
        return _silu(acc)

    row = lax.broadcasted_iota(I32, (q, q), 0)
    col = lax.broadcasted_iota(I32, (q, q), 1)
    causal = row >= col
    dt = _softplus(dq[...] + dtb_ref[...])
    if tt < q:
        dt = jnp.where(lax.broadcasted_iota(I32, (q, LANES), 0) < tt, dt, 0.0)
    da = dt * a_ref[...]
    cum = jnp.dot(causal.astype(F32), da, precision=lax.Precision.HIGHEST, preferred_element_type=F32)
    cum_t = cum.T
    dt_t = dt.T
    cum_last = cum[q - 1:q, :]
    ecum = jnp.exp(cum)
    dtde = dt * jnp.exp(cum_last - cum)
    cd = jnp.exp(cum_last)
    lo = lax.broadcasted_iota(I32, (q, LANES), 1) < SSD_P
    lo1 = lo[0:1, :]
    nt = (((1,), (1,)), ((), ()))
    pairs_per_group = SSD_HEADS // SSD_GROUPS // 2

    for g in range(SSD_GROUPS):
        bm = conv_cols(SSD_D_INNER + g * SSD_N, SSD_D_INNER + (g + 1) * SSD_N)
        cm = conv_cols(SSD_D_INNER + SSD_GN + g * SSD_N, SSD_D_INNER + SSD_GN + (g + 1) * SSD_N)
        bmb, cmb = bm.astype(BF16), cm.astype(BF16)
        cb = lax.dot_general(cmb, bmb, nt, preferred_element_type=F32)
        bm_t = bm.T.astype(BF16)
        for pp in range(pairs_per_group):
            p = g * pairs_per_group + pp
            h0, h1 = 2 * p, 2 * p + 1
            cs = slice(p * LANES, (p + 1) * LANES)
            xs = conv_cols(p * LANES, (p + 1) * LANES)
            xsb = xs.astype(BF16)
            yd = []
            for h in (h0, h1):
                seg = cum[:, h:h + 1] - cum_t[h:h + 1, :]
                m = cb * jnp.where(causal, jnp.exp(jnp.where(causal, seg, 0.0)), 0.0) * dt_t[h:h + 1, :]
                yd.append(jnp.dot(m.astype(BF16), xsb, preferred_element_type=F32))
            s_old = S[:, cs]
            yo = jnp.dot(cmb, s_old.astype(BF16), preferred_element_type=F32)
            yo = yo * jnp.where(lo, ecum[:, h0:h0 + 1], ecum[:, h1:h1 + 1])
            y = jnp.where(lo, yd[0], yd[1]) + yo + dsk_ref[:, cs] * xs
            y_ref[:, cs] = y[0:tt, :]
            wts = jnp.where(lo, dtde[:, h0:h0 + 1], dtde[:, h1:h1 + 1])
            upd = jnp.dot(bm_t, (xs * wts).astype(BF16), preferred_element_type=F32)
            S[:, cs] = s_old * jnp.where(lo1, cd[:, h0:h0 + 1], cd[:, h1:h1 + 1]) + upd

    @pl.when(c == pl.num_programs(1) - 1)
    def _():
        hT_ref[...] = S[...]


def _ssd(xbc, dt, cbuf, h0t, cw, cb, dtb, a, dsk):
    bsz, t, _ = xbc.shape
    tt = min(t, SSD_Q)
    row = lambda b, c: (b, c, 0)
    per_b = lambda b, c: (b, 0, 0)
    fixed = lambda b, c: (0, 0)
    return pl.pallas_call(
        functools.partial(_ssd_kernel, tt=tt),
        grid=(bsz, t // tt),
        in_specs=[pl.BlockSpec((None, tt, SSD_CONV_DIM), row), pl.BlockSpec((None, tt, LANES), row),
                  pl.BlockSpec((None, CONV_W - 1, SSD_CONV_DIM), per_b),
                  pl.BlockSpec((None, SSD_N, SSD_D_INNER), per_b),
                  pl.BlockSpec((CONV_W, SSD_CONV_DIM), fixed), pl.BlockSpec((1, SSD_CONV_DIM), fixed),
                  pl.BlockSpec((1, LANES), fixed), pl.BlockSpec((1, LANES), fixed),
                  pl.BlockSpec((1, SSD_D_INNER), fixed)],
        out_specs=[pl.BlockSpec((None, tt, SSD_D_INNER), row),
                   pl.BlockSpec((None, CONV_W - 1, SSD_CONV_DIM), per_b),
                   pl.BlockSpec((None, SSD_N, SSD_D_INNER), per_b)],
        out_shape=[jax.ShapeDtypeStruct((bsz, t, SSD_D_INNER), F32),
                   jax.ShapeDtypeStruct((bsz, CONV_W - 1, SSD_CONV_DIM), F32),
                   jax.ShapeDtypeStruct((bsz, SSD_N, SSD_D_INNER), F32)],
        scratch_shapes=[pltpu.VMEM((8 + SSD_Q, SSD_CONV_DIM), F32), pltpu.VMEM((SSD_Q, LANES), F32),
                        pltpu.VMEM((SSD_N, SSD_D_INNER), F32)],
        compiler_params=_cp("parallel", "arbitrary"),
        name="ssd",
    )(xbc, dt, cbuf, h0t, cw, cb, dtb, a, dsk)


def _outproj_odd_kernel(*refs):
    y_ref, z_ref, x_ref, nw_ref, w_ref, g_ref, b_ref = refs[:7]
    o_ref = refs[-1]
    gw = SSD_D_INNER // SSD_GROUPS
    acc = None
    for g in range(SSD_GROUPS):
        cs = slice(g * gw, (g + 1) * gw)
        v = y_ref[:, cs] * _silu(z_ref[:, cs])
        v = v * lax.rsqrt(jnp.mean(v * v, -1, keepdims=True) + RMS_EPS) * nw_ref[:, cs]
        part = jnp.dot(v.astype(BF16), w_ref[cs, :], preferred_element_type=F32)
        acc = part if acc is None else acc + part
    o_ref[...] = _ln_rows(ALPHA * x_ref[...] + acc, g_ref[...], b_ref[...])


def _router_kernel(x_ref, w_ref, b_ref, idx_ref, gate_ref, rank_ref, cnt_ref, carry, *, tm):
    i = pl.program_id(0)

    @pl.when(i == 0)
    def _():
        carry[...] = jnp.zeros(carry.shape, F32)

    logits = jnp.dot(x_ref[...], w_ref[...], precision=lax.Precision.HIGHEST, preferred_element_type=F32) + b_ref[...]
    lane = lax.broadcasted_iota(I32, (tm, N_EXPERTS), 1)
    work = logits
    vals, hots = [], []
    for _ in range(TOP_K):
        m = jnp.max(work, axis=1, keepdims=True)
        ik = jnp.min(jnp.where(work == m, lane, N_EXPERTS), axis=1, keepdims=True)
        hot = lane == ik
        vals.append(m)
        hots.append(hot)
        work = jnp.where(hot, NEG_INF, work)
    es = [jnp.exp(v - vals[0]) for v in vals]
    den = es[0] + es[1] + es[2] + es[3]
    hot_all = (hots[0] | hots[1] | hots[2] | hots[3])
    r_i = lax.broadcasted_iota(I32, (tm, tm), 0)
    c_i = lax.broadcasted_iota(I32, (tm, tm), 1)
    before = (r_i > c_i).astype(BF16)
    pre = jnp.dot(before, hot_all.astype(BF16), preferred_element_type=F32) + carry[...]
    lane4 = lax.broadcasted_iota(I32, (tm, TOP_K), 1)
    idx = jnp.zeros((tm, TOP_K), I32)
    gate = jnp.zeros((tm, TOP_K), F32)
    rank = jnp.zeros((tm, TOP_K), I32)
    for k in range(TOP_K):
        ik = jnp.sum(jnp.where(hots[k], lane, 0), axis=1, keepdims=True)
        rk = jnp.sum(jnp.where(hots[k], pre, 0.0), axis=1, keepdims=True).astype(I32)
        idx = jnp.where(lane4 == k, ik, idx)
        rank = jnp.where(lane4 == k, rk, rank)
        gate = jnp.where(lane4 == k, es[k] / den, gate)
    idx_ref[...] = idx
    gate_ref[...] = gate
    rank_ref[...] = rank
    carry[...] = carry[...] + jnp.sum(hot_all.astype(F32), axis=0, keepdims=True)
    cnt_ref[...] = carry[...]


def _router(x_all, w, b):
    n = x_all.shape[0]
    tm = 384 if n % 384 == 0 else 128
    out4 = lambda dt: jax.ShapeDtypeStruct((n, TOP_K), dt)
    return pl.pallas_call(
        functools.partial(_router_kernel, tm=tm),
        grid=(n // tm,),
        in_specs=[pl.BlockSpec((tm, D_MODEL), lambda i: (i, 0)),
                  pl.BlockSpec((D_MODEL, N_EXPERTS), lambda i: (0, 0)),
                  pl.BlockSpec((1, N_EXPERTS), lambda i: (0, 0))],
        out_specs=[pl.BlockSpec((tm, TOP_K), lambda i: (i, 0))] * 3 + [pl.BlockSpec((1, N_EXPERTS), lambda i: (0, 0))],
        out_shape=[out4(I32), out4(F32), out4(I32), jax.ShapeDtypeStruct((1, N_EXPERTS), F32)],
        scratch_shapes=[pltpu.VMEM((1, N_EXPERTS), F32)],
        compiler_params=_cp("arbitrary"),
        name="router",
    )(x_all, w, b)


def _row_copy(src_hbm, row, dst, dst_row, sem):
    return pltpu.make_async_copy(src_hbm.at[pl.ds(row, 1), :], dst.at[pl.ds(dst_row, 1), :], sem)


def _ffn_kernel(te_ref, tos_ref, nv_ref, x_hbm, w1_ref, b1_ref, w2_ref, b2_ref, o_ref, xbuf, sem, *, rows):
    i = pl.program_id(0)
    nvalid = nv_ref[0]
    slot = i % 2

    def gather(tile, s):
        def issue(r, _):
            _row_copy(x_hbm, tos_ref[tile * rows + r], xbuf.at[s], r, sem.at[s]).start()
            return 0
        lax.fori_loop(0, rows, issue, 0, unroll=8)

    @pl.when(i == 0)
    def _():
        gather(0, 0)

    @pl.when(i + 1 < nvalid)
    def _():
        gather(i + 1, 1 - slot)

    @pl.when(i < nvalid)
    def _():
        pltpu.make_async_copy(x_hbm.at[pl.ds(0, rows), :], xbuf.at[slot], sem.at[slot]).wait()
        xb = xbuf[slot].astype(BF16)
        h = jnp.dot(xb, w1_ref[...], preferred_element_type=F32) + b1_ref[...]
        even = (lax.broadcasted_iota(I32, (rows, LANES), 1) % 2) == 0
        parts = []
        for c in range(D_FF // LANES):
            ha = h[:, c * LANES:(c + 1) * LANES]
            hb = h[:, D_FF + c * LANES:D_FF + (c + 1) * LANES]
            glu = jnp.minimum(jnp.where(even, ha, pltpu.roll(hb, 1, 1)), SWIGLU_LIMIT)
            lin = jnp.clip(jnp.where(even, pltpu.roll(ha, LANES - 1, 1), hb), -SWIGLU_LIMIT, SWIGLU_LIMIT)
            parts.append((glu * _sigmoid(SWIGLU_ALPHA * glu) * (lin + 1.0)).astype(BF16))
        act = jnp.concatenate(parts, axis=1)
        o_ref[...] = jnp.dot(act, w2_ref[...], preferred_element_type=F32) + b2_ref[...]

    @pl.when(i >= nvalid)
    def _():
        o_ref[...] = jnp.zeros(o_ref.shape, F32)


def _ffn(x_all, tile_expert, tok_of_slot, nvalid, w1, b1, w2, b2, ntiles):
    rows = MOE_ROWS
    wspec = lambda shape: pl.BlockSpec((None,) + shape, lambda i, te, tos, nv: (te[i], 0, 0))
    grid_spec = pltpu.PrefetchScalarGridSpec(
        num_scalar_prefetch=3,
        grid=(ntiles,),
        in_specs=[pl.BlockSpec(memory_space=pl.ANY),
                  wspec((D_MODEL, 2 * D_FF)), wspec((1, 2 * D_FF)),
                  wspec((D_FF, D_MODEL)), wspec((1, D_MODEL))],
        out_specs=pl.BlockSpec((rows, D_MODEL), lambda i, te, tos, nv: (i, 0)),
        scratch_shapes=[pltpu.VMEM((2, rows, D_MODEL), F32), pltpu.SemaphoreType.DMA((2,))],
    )
    return pl.pallas_call(
        functools.partial(_ffn_kernel, rows=rows),
        grid_spec=grid_spec,
        out_shape=jax.ShapeDtypeStruct((ntiles * rows, D_MODEL), F32),
        compiler_params=_cp("arbitrary"),
        name="moe_ffn",
    )(tile_expert, tok_of_slot, nvalid, x_all, w1, b1, w2, b2)


def _combine_kernel(pos_ref, o_hbm, gate_ref, x_ref, g_ref, b_ref, out_ref, buf, sem, *, tm):
    i = pl.program_id(0)
    n = pl.num_programs(0)
    slot = i % 2

    def gather(tile, s):
        def issue(t, _):
            for k in range(TOP_K):
                _row_copy(o_hbm, pos_ref[(tile * tm + t) * TOP_K + k], buf.at[s, k], t, sem.at[s]).start()
            return 0
        lax.fori_loop(0, tm, issue, 0, unroll=4)

    @pl.when(i == 0)
    def _():
        gather(0, 0)

    @pl.when(i + 1 < n)
    def _():
        gather(i + 1, 1 - slot)

    f = None
    for k in range(TOP_K):
        pltpu.make_async_copy(o_hbm.at[pl.ds(0, tm), :], buf.at[slot, k], sem.at[slot]).wait()
    for k in range(TOP_K):
        part = gate_ref[:, k:k + 1] * buf[slot, k]
        f = part if f is None else f + part
    out_ref[...] = _ln_rows(ALPHA * x_ref[...] + f, g_ref[...], b_ref[...])


def _combine(pos_flat, o_sorted, gate, x_all, g, b):
    n = x_all.shape[0]
    tm = 128
    grid_spec = pltpu.PrefetchScalarGridSpec(
        num_scalar_prefetch=1,
        grid=(n // tm,),
        in_specs=[pl.BlockSpec(memory_space=pl.ANY),
                  pl.BlockSpec((tm, TOP_K), lambda i, pos: (i, 0)),
                  pl.BlockSpec((tm, D_MODEL), lambda i, pos: (i, 0)),
                  pl.BlockSpec((1, D_MODEL), lambda i, pos: (0, 0)),
                  pl.BlockSpec((1, D_MODEL), lambda i, pos: (0, 0))],
        out_specs=pl.BlockSpec((tm, D_MODEL), lambda i, pos: (i, 0)),
        scratch_shapes=[pltpu.VMEM((2, TOP_K, tm, D_MODEL), F32), pltpu.SemaphoreType.DMA((2,))],
    )
    return pl.pallas_call(
        functools.partial(_combine_kernel, tm=tm),
        grid_spec=grid_spec,
        out_shape=jax.ShapeDtypeStruct((n, D_MODEL), F32),
        compiler_params=_cp("arbitrary"),
        name="moe_combine",
    )(pos_flat, o_sorted, gate, x_all, g, b)


def _moe(x_all, w_r, b_r, w1, b1, w2, b2, g, b):
    n = x_all.shape[0]
    rows = MOE_ROWS
    ntiles = -(-(n * TOP_K + N_EXPERTS * (rows - 1)) // rows)
    idx, gate, rank, cnt = _router(x_all, w_r, b_r.reshape(1, N_EXPERTS))
    counts = cnt[0].astype(I32)
    padded = ((counts + rows - 1) // rows) * rows
    ends = jnp.cumsum(padded)
    starts = ends - padded
    experts = jnp.arange(N_EXPERTS, dtype=I32)
    pos = rank + jnp.sum(jnp.where(idx[:, :, None] == experts, starts, 0), axis=-1)
    tile_starts = jnp.arange(ntiles, dtype=I32) * rows
    tile_expert = jnp.minimum(jnp.sum((tile_starts[:, None] >= ends[None, :]).astype(I32), axis=1), N_EXPERTS - 1)
    nvalid = (ends[-1:] // rows).astype(I32)
    tok = jnp.broadcast_to(jnp.arange(n, dtype=I32)[:, None], (n, TOP_K))
    tok_of_slot = jnp.zeros((ntiles * rows,), I32).at[pos.reshape(-1)].set(tok.reshape(-1))
    half = D_FF // 2
    w2p = jnp.transpose(w2.reshape(N_EXPERTS, 2, half, D_MODEL), (0, 2, 1, 3)).reshape(N_EXPERTS, D_FF, D_MODEL)
    o_sorted = _ffn(x_all, tile_expert, tok_of_slot, nvalid, w1.astype(BF16), b1[:, None, :],
                    w2p.astype(BF16), b2[:, None, :], ntiles)
    return _combine(pos.reshape(-1), o_sorted, gate, x_all, g.reshape(1, -1), b.reshape(1, -1))


def _block_diag(w):
    nb, bw, _ = w.shape
    eye = jnp.eye(nb, dtype=w.dtype)
    return (eye[:, None, :, None] * w[:, :, None, :]).reshape(nb * bw, nb * bw)


def _mix_even(x_arr, out_row0, n_total, bsz, t, rg_conv, rg_h, cache, p, prev):
    n = bsz * t
    xa, ga, q, kv = _inproj_even(x_arr, 0, n, p["w_in"])
    ya, conv_new, h_last = _rglru(
        xa.reshape(bsz, t, RG_WIDTH), ga.reshape(bsz, t, RG_WIDTH), rg_conv, rg_h.reshape(bsz, 1, RG_WIDTH),
        p["rg_conv_w"], p["rg_conv_b"], p["rg_wg"], p["rg_bg"], p["rg_c"])
    q3 = q.reshape(bsz, t, B_WIDTH)
    kv3 = kv.reshape(bsz, t, 2 * B_WIDTH)
    if cache is None:
        yb = _attn_prompt(q3, kv3, p["rel_bias"])
    else:
        yb = _attn_sample(q3, cache, kv3, p["rel_bias"])
    x1 = _token_call(_outproj_even_kernel, "outproj_even", n, 0, n_total, out_row0,
                     [ya.reshape(n, RG_WIDTH), yb.reshape(n, B_WIDTH)], x_arr,
                     [p["w_out"], p["ln_g0"], p["ln_b0"]], prev)
    return x1, conv_new, h_last.reshape(bsz, RG_WIDTH), kv3


def _mix_odd(x_all, row0, bsz, t, ssd_conv, ssd_h, p, prev):
    n = bsz * t
    z, xbc, dt = _inproj_odd(x_all, row0, n, p["ssd_w_main"], p["ssd_w_dt"])
    h0t = jnp.transpose(ssd_h.astype(F32), (0, 3, 1, 2)).reshape(bsz, SSD_N, SSD_D_INNER)
    y, conv_new, h_t = _ssd(xbc.reshape(bsz, t, SSD_CONV_DIM), dt.reshape(bsz, t, LANES), ssd_conv, h0t,
                            p["ssd_conv_w"], p["ssd_conv_b"], p["ssd_dtb"], p["ssd_a"], p["ssd_dsk"])
    h_last = jnp.transpose(h_t.reshape(bsz, SSD_N, SSD_HEADS, SSD_P), (0, 2, 3, 1))
    x1 = _token_call(_outproj_odd_kernel, "outproj_odd", n, row0, x_all.shape[0], row0,
                     [y.reshape(n, SSD_D_INNER), z], x_all,
                     [p["ssd_norm_w"], p["ssd_w_out"], p["ln_g2"], p["ln_b2"]], prev)
    return x1, conv_new, h_last


def kernel(x_prompt, x_sample, state_rglru_conv, state_rglru_h, cache_swa_kv, state_ssd_conv, state_ssd_h,
           rel_bias, w_in_mix, rg_conv_w, rg_conv_b, rg_w_a, rg_b_a, rg_w_i, rg_b_i, rg_lambda, w_out_mix,
           ssd_w_in, ssd_conv_w, ssd_conv_b, ssd_dt_bias, ssd_a_log, ssd_d, ssd_norm_w, ssd_w_out,
           ln_g, ln_b, router_w, router_b, exp_w1, exp_b1, exp_w2, exp_b2):
    bp, tp, _ = x_prompt.shape
    bs, ts, _ = x_sample.shape
    n_p, n_s = bp * tp, bs * ts
    n_total = n_p + n_s
    row2 = lambda v: v.reshape(1, -1).astype(F32)
    pad_heads = lambda v: jnp.pad(v.astype(F32), (0, LANES - SSD_HEADS)).reshape(1, LANES)

    p = dict(
        rel_bias=rel_bias,
        w_in=w_in_mix[0].astype(BF16),
        rg_conv_w=rg_conv_w[0].astype(F32), rg_conv_b=row2(rg_conv_b[0]),
        rg_wg=jnp.concatenate([_block_diag(rg_w_a[0]), _block_diag(rg_w_i[0])], axis=1).astype(BF16),
        rg_bg=jnp.concatenate([rg_b_a[0].reshape(1, -1), rg_b_i[0].reshape(1, -1)], axis=1).astype(F32),
        rg_c=row2(-RG_C * jax.nn.softplus(-rg_lambda[0].astype(F32))),
        w_out=w_out_mix[0].astype(BF16),
        ln_g0=row2(ln_g[0, 0]), ln_b0=row2(ln_b[0, 0]),
        ssd_w_main=ssd_w_in[0][:, :SSD_MAIN].astype(BF16),
        ssd_w_dt=jnp.pad(ssd_w_in[0][:, SSD_MAIN:], ((0, 0), (0, LANES - SSD_HEADS))).astype(BF16),
        ssd_conv_w=ssd_conv_w[0].astype(F32), ssd_conv_b=row2(ssd_conv_b[0]),
        ssd_dtb=pad_heads(ssd_dt_bias[0]), ssd_a=pad_heads(-jnp.exp(ssd_a_log[0].astype(F32))),
        ssd_dsk=row2(jnp.repeat(ssd_d[0].astype(F32), SSD_P)),
        ssd_norm_w=row2(ssd_norm_w[0]), ssd_w_out=ssd_w_out[0].astype(BF16),
        ln_g2=row2(ln_g[1, 0]), ln_b2=row2(ln_b[1, 0]),
    )

    xp = x_prompt.reshape(n_p, D_MODEL)
    xs = x_sample.reshape(n_s, D_MODEL)
    zeros = lambda *s: jnp.zeros(s, F32)
    x1, p_rg_conv, p_rg_h, p_kv = _mix_even(xp, 0, n_total, bp, tp, zeros(bp, CONV_W - 1, RG_WIDTH),
                                            zeros(bp, RG_WIDTH), None, p, None)
    cache = cache_swa_kv[0].reshape(bs, cache_swa_kv.shape[2], 2 * B_WIDTH)
    x1, s_rg_conv, s_rg_h, s_kv = _mix_even(xs, n_p, n_total, bs, ts, state_rglru_conv[0], state_rglru_h[0],
                                            cache, p, x1)
    x2 = _moe(x1, router_w[0].astype(F32), router_b[0].astype(F32), exp_w1[0], exp_b1[0], exp_w2[0], exp_b2[0],
              ln_g[0, 1], ln_b[0, 1])

    x3, p_ssd_conv, p_ssd_h = _mix_odd(x2, 0, bp, tp, zeros(bp, CONV_W - 1, SSD_CONV_DIM),
                                       zeros(bp, SSD_HEADS, SSD_P, SSD_N), p, None)
    x3, s_ssd_conv, s_ssd_h = _mix_odd(x2, n_p, bs, ts, state_ssd_conv[0], state_ssd_h[0], p, x3)
    x4 = _moe(x3, router_w[1].astype(F32), router_b[1].astype(F32), exp_w1[1], exp_b1[1], exp_w2[1], exp_b2[1],
              ln_g[1, 1], ln_b[1, 1])

    kv_shape = lambda b, t: (1, b, t, 2, B_HEADS, B_HD)
    return (x4[:n_p].reshape(bp, tp, D_MODEL), x4[n_p:].reshape(bs, ts, D_MODEL),
            p_rg_conv[None], p_rg_h[None], p_kv[:, -min(MAX_WINDOW, tp):].reshape(kv_shape(bp, min(MAX_WINDOW, tp))),
            p_ssd_conv[None], p_ssd_h[None],
            s_rg_conv[None], s_rg_h[None], s_kv.reshape(kv_shape(bs, ts)),
            s_ssd_conv[None], s_ssd_h[None])
```

```python
import functools
import math

import numpy as np
import jax
import jax.numpy as jnp
from jax import lax
from jax.experimental import pallas as pl
from jax.experimental.pallas import tpu as pltpu

F32 = jnp.float32
BF16 = jnp.bfloat16
I32 = jnp.int32

D_MODEL = 1024
DEPTH = 2
ALPHA = (2 * DEPTH) ** 0.25
LN_EPS = 1e-5
RMS_EPS = 1e-5
CONV_W = 4
RG_BLOCKS = 12
RG_BW = 64
RG_WIDTH = RG_BLOCKS * RG_BW
RG_C = 8.0
B_HEADS = 12
B_HD = 64
B_WIDTH = B_HEADS * B_HD
PATTERNS = ((128, 1), (512, 4), (2048, 16))
MAX_WINDOW = 2048
N_BUCKETS = 32
MAX_DISTANCE = MAX_WINDOW
EVEN_IN = 2 * RG_WIDTH + 3 * B_WIDTH
SSD_D_INNER = 2 * D_MODEL
SSD_P = 64
SSD_HEADS = SSD_D_INNER // SSD_P
SSD_N = 128
SSD_GROUPS = 4
SSD_GN = SSD_GROUPS * SSD_N
SSD_CONV_DIM = SSD_D_INNER + 2 * SSD_GN
SSD_MAIN = SSD_D_INNER + SSD_CONV_DIM
N_EXPERTS = 32
TOP_K = 4
D_FF = D_MODEL
SWIGLU_LIMIT = 7.0
SWIGLU_ALPHA = 1.702

LANES = 128
VMEM_LIMIT = 56 * 1024 * 1024

SSD_Q = 128
ATT_BLK = 256
ATT_CHUNK = 64
MOE_ROWS = 256
NEG_INF = float("-inf")


ROW_TILES = D_MODEL // LANES


def _cp(*sem):
    return pltpu.CompilerParams(dimension_semantics=sem, vmem_limit_bytes=VMEM_LIMIT)


def _store_row_tiles(ref, val):
    rows = val.shape[0]
    for s in range(ROW_TILES):
        ref[pl.ds(s, rows, stride=ROW_TILES), :] = val[:, s * LANES:(s + 1) * LANES]


def _load_row_tiles(ref, rows):
    return jnp.concatenate([ref[pl.ds(s, rows, stride=ROW_TILES), :] for s in range(ROW_TILES)], axis=1)


def _ln_rows(v, g, b):
    mu = jnp.mean(v, -1, keepdims=True)
    xc = v - mu
    var = jnp.mean(xc * xc, -1, keepdims=True)
    return xc * lax.rsqrt(var + LN_EPS) * g + b


def _bf16_round(x):
    return x.astype(BF16).astype(F32)


def _sigmoid(x):
    return 1.0 / (1.0 + jnp.exp(-x))


def _silu(x):
    return x * _sigmoid(x)


def _softplus(x):
    return jnp.maximum(x, 0.0) + jnp.log1p(jnp.exp(-jnp.abs(x)))


def _gelu_tanh(x):
    return 0.5 * x * (1.0 + jnp.tanh(math.sqrt(2.0 / math.pi) * (x + 0.044715 * (x * x * x))))


def _row_tile(n):
    return min(n, 256)


def _inproj_even_kernel(x_ref, w_ref, xa_ref, ga_ref, q_ref, kv_ref):
    xb = x_ref[...].astype(BF16)
    c0, c1, c2 = RG_WIDTH, 2 * RG_WIDTH, 2 * RG_WIDTH + B_WIDTH
    xa_ref[...] = jnp.dot(xb, w_ref[:, 0:c0], preferred_element_type=F32)
    ga_ref[...] = jnp.dot(xb, w_ref[:, c0:c1], preferred_element_type=F32)
    q_ref[...] = jnp.dot(xb, w_ref[:, c1:c2], preferred_element_type=F32)
    kv_ref[...] = jnp.dot(xb, w_ref[:, c2:EVEN_IN], preferred_element_type=F32)


def _inproj_even(x_all, row0, n, w):
    tm = _row_tile(n)
    off = row0 // tm
    widths = (RG_WIDTH, RG_WIDTH, B_WIDTH, 2 * B_WIDTH)
    return pl.pallas_call(
        _inproj_even_kernel,
        grid=(n // tm,),
        in_specs=[pl.BlockSpec((tm, D_MODEL), lambda i: (i + off, 0)),
                  pl.BlockSpec((D_MODEL, EVEN_IN), lambda i: (0, 0))],
        out_specs=[pl.BlockSpec((tm, c), lambda i: (i, 0)) for c in widths],
        out_shape=[jax.ShapeDtypeStruct((n, c), F32) for c in widths],
        compiler_params=_cp("parallel"),
        name="inproj_even",
    )(x_all, w)


def _rglru_kernel(xa_ref, ga_ref, cbuf_ref, h0_ref, cw_ref, cb_ref, wg_ref, bg_ref, c_ref,
                  ya_ref, cnew_ref, hlast_ref, xx, a_s, u_s, hc, *, tt):
    j = pl.program_id(1)
    hist = CONV_W - 1
    base = 8

    @pl.when(j == 0)
    def _():
        xx[base - hist:base, :] = cbuf_ref[...]
        hc[...] = h0_ref[...]

    @pl.when(j > 0)
    def _():
        xx[base - hist:base, :] = xx[base + tt - hist:base + tt, :]

    xx[base:base + tt, :] = xa_ref[...]
    conv = cb_ref[...]
    for k in range(CONV_W):
        conv = conv + cw_ref[k:k + 1, :] * _bf16_round(xx[base - hist + k:base - hist + k + tt, :])
    gates = jnp.dot(conv.astype(BF16), wg_ref[...], preferred_element_type=F32) + bg_ref[...]
    r = _sigmoid(gates[:, :RG_WIDTH])
    ig = _sigmoid(gates[:, RG_WIDTH:])
    log_a = c_ref[...] * r
    a = jnp.exp(log_a)
    a_s[...] = a
    u_s[...] = jnp.sqrt(-jnp.tanh(log_a) * (a * a + 1.0)) * (ig * conv)

    def step(t, h):
        h = a_s[pl.ds(t, 1), :] * h + u_s[pl.ds(t, 1), :]
        u_s[pl.ds(t, 1), :] = h
        return h

    h = lax.fori_loop(0, tt, step, hc[...], unroll=min(tt, 8))
    hc[...] = h
    ya_ref[...] = _gelu_tanh(ga_ref[...]) * u_s[...]

    @pl.when(j == pl.num_programs(1) - 1)
    def _():
        hlast_ref[...] = h
        cnew_ref[...] = xx[base + tt - hist:base + tt, :]


def _rglru(xa, ga, cbuf, h0, cw, cb, wg, bg, cvec):
    bsz, t, c = xa.shape
    tt = min(t, 256)
    row = lambda b, j: (b, j, 0)
    per_b = lambda b, j: (b, 0, 0)
    fixed = lambda b, j: (0, 0)
    return pl.pallas_call(
        functools.partial(_rglru_kernel, tt=tt),
        grid=(bsz, t // tt),
        in_specs=[pl.BlockSpec((None, tt, c), row), pl.BlockSpec((None, tt, c), row),
                  pl.BlockSpec((None, CONV_W - 1, c), per_b), pl.BlockSpec((None, 1, c), per_b),
                  pl.BlockSpec((CONV_W, c), fixed), pl.BlockSpec((1, c), fixed),
                  pl.BlockSpec((c, 2 * c), fixed), pl.BlockSpec((1, 2 * c), fixed),
                  pl.BlockSpec((1, c), fixed)],
        out_specs=[pl.BlockSpec((None, tt, c), row), pl.BlockSpec((None, CONV_W - 1, c), per_b),
                   pl.BlockSpec((None, 1, c), per_b)],
        out_shape=[jax.ShapeDtypeStruct((bsz, t, c), F32),
                   jax.ShapeDtypeStruct((bsz, CONV_W - 1, c), F32),
                   jax.ShapeDtypeStruct((bsz, 1, c), F32)],
        scratch_shapes=[pltpu.VMEM((8 + tt, c), F32), pltpu.VMEM((tt, c), F32),
                        pltpu.VMEM((tt, c), F32), pltpu.VMEM((1, c), F32)],
        compiler_params=_cp("parallel", "arbitrary"),
        name="rglru",
    )(xa, ga, cbuf, h0, cw, cb, wg, bg, cvec)


def _t5_bucket(dist):
    max_exact = N_BUCKETS // 2
    safe = np.maximum(dist, 1)
    large = max_exact + (np.log(safe / max_exact) / np.log(MAX_DISTANCE / max_exact)
                         * (N_BUCKETS - max_exact)).astype(np.int32)
    return np.where(dist < max_exact, dist, np.minimum(large, N_BUCKETS - 1)).astype(np.int32)


def _distance_bias(rel_bias, max_d):
    d = np.arange(max_d + 1)
    mult = np.zeros(max_d + 1, np.float64)
    for w, dil in PATTERNS:
        mult += ((d % dil == 0) & (d <= w)).astype(np.float64)
    logm = np.where(mult > 0, np.log(np.maximum(mult, 1.0)), -np.inf).astype(np.float32)
    vals = rel_bias.astype(F32)[_t5_bucket(d)].T + jnp.asarray(logm)[None, :]
    return jnp.concatenate([vals, jnp.full((vals.shape[0], 1), NEG_INF, F32)], axis=1)


def _attn_prompt_kernel(q_ref, k_ref, v_ref, rv_ref, o_ref, tab_ref, k_s, v_s, *, blk, nq):
    qi = pl.program_id(2)

    @pl.when((pl.program_id(1) == 0) & (qi == 0))
    def _():
        for h in range(2):
            for d in range(nq):
                win = rv_ref[h:h + 1, (nq - 1 - d) * blk:(nq + 1 - d) * blk]
                rolled = pltpu.roll(jnp.broadcast_to(win, (blk, 2 * blk)), 0, 1, stride=1, stride_axis=0)
                tab_ref[h, d] = rolled[:, blk:]

    lo = lax.broadcasted_iota(I32, (blk, LANES), 1) < B_HD

    @pl.when(qi == 0)
    def _():
        k_s[...] = k_ref[...].astype(BF16)
        v_s[...] = v_ref[...].astype(BF16)

    q2 = q_ref[...] * (B_HD ** -0.5)
    qh = (jnp.where(lo, q2, 0.0).astype(BF16), jnp.where(lo, 0.0, q2).astype(BF16))
    nt = (((1,), (1,)), ((), ()))
    ch = ATT_CHUNK

    def scores(d):
        k = k_s[pl.ds(pl.multiple_of((qi - d) * blk, blk), blk), :]
        return tuple(lax.dot_general(qh[h], k, nt, preferred_element_type=F32) for h in range(2))

    def body(d, carry):
        m0, l0, m1, l1, acc, s0, s1 = carry
        s_next = scores(jnp.minimum(d + 1, qi))
        v = v_s[pl.ds(pl.multiple_of((qi - d) * blk, blk), blk), :]
        new = []
        for h, (m, l, s) in enumerate(((m0, l0, s0), (m1, l1, s1))):
            mns, lns, als, ps = [], [], [], []
            for c in range(blk // ch):
                rs = slice(c * ch, (c + 1) * ch)
                sc = s[rs] + tab_ref[h, d, rs, :]
                mn = jnp.maximum(m[rs], jnp.broadcast_to(jnp.max(sc, axis=1, keepdims=True), (ch, LANES)))
                p = jnp.concatenate([jnp.exp(sc[:, j * LANES:(j + 1) * LANES] - mn) for j in range(blk // LANES)], axis=1)
                al = jnp.exp(m[rs] - mn)
                mns.append(mn)
                als.append(al)
                lns.append(al * l[rs] + jnp.broadcast_to(jnp.sum(p, axis=1, keepdims=True), (ch, LANES)))
                ps.append(p.astype(BF16))
            cat = lambda xs: jnp.concatenate(xs, axis=0)
            new.append((cat(mns), cat(lns), cat(als), jnp.dot(cat(ps), v, preferred_element_type=F32)))
        (m0, l0, a0, pv0), (m1, l1, a1, pv1) = new
        acc = jnp.where(lo, a0, a1) * acc + jnp.where(lo, pv0, pv1)
        return (m0, l0, m1, l1, acc) + s_next

    minf = jnp.full((blk, LANES), NEG_INF, F32)
    zero = jnp.zeros((blk, LANES), F32)
    out = lax.fori_loop(0, qi + 1, body, (minf, zero, minf, zero, zero) + scores(0))
    m0, l0, m1, l1, acc = out[:5]
    o_ref[...] = acc / jnp.where(lo, l0, l1)


def _attn_prompt(q, kv, rel_bias):
    bsz, t, _ = q.shape
    blk = min(ATT_BLK, t)
    nq = t // blk
    npair = B_HEADS // 2
    vals = _distance_bias(rel_bias, t)[:, :t + 1]
    rv = jnp.concatenate([jnp.full((B_HEADS, blk - 1), NEG_INF, F32), vals], axis=1)[:, ::-1]
    rv = rv.reshape(npair, 2, t + blk)
    return pl.pallas_call(
        functools.partial(_attn_prompt_kernel, blk=blk, nq=nq),
        grid=(npair, bsz, nq),
        in_specs=[pl.BlockSpec((None, blk, LANES), lambda p, b, i: (b, i, p)),
                  pl.BlockSpec((None, t, LANES), lambda p, b, i: (b, 0, p)),
                  pl.BlockSpec((None, t, LANES), lambda p, b, i: (b, 0, npair + p)),
                  pl.BlockSpec((None, 2, t + blk), lambda p, b, i: (p, 0, 0))],
        out_specs=pl.BlockSpec((None, blk, LANES), lambda p, b, i: (b, i, p)),
        out_shape=jax.ShapeDtypeStruct((bsz, t, B_WIDTH), F32),
        scratch_shapes=[pltpu.VMEM((2, nq, blk, blk), F32), pltpu.VMEM((t, LANES), BF16),
                        pltpu.VMEM((t, LANES), BF16)],
        compiler_params=_cp("arbitrary", "arbitrary", "arbitrary"),
        name="attn_prompt",
    )(q, kv, kv, rv)


ATT_S_ROWS = 16


def _attn_sample_kernel(qbd_ref, kc_ref, vc_ref, kn_ref, vn_ref, tabc_ref, tabn_ref, mask_ref, o_ref,
                        m_s, l_s, acc_s, *, t_new):
    c = pl.program_id(1)
    rows = t_new * ATT_S_ROWS
    nt = (((1,), (1,)), ((), ()))

    @pl.when(c == 0)
    def _():
        m_s[...] = jnp.full((rows, 1), NEG_INF, F32)
        l_s[...] = jnp.zeros((rows, 1), F32)
        acc_s[...] = jnp.zeros((rows, B_WIDTH), F32)

    qf = qbd_ref[...] * (B_HD ** -0.5)
    qb = qf.astype(BF16)
    s = lax.dot_general(qb, kc_ref[...].astype(BF16), nt, preferred_element_type=F32) + tabc_ref[...]
    m = m_s[...]
    mn = jnp.maximum(m, jnp.max(s, axis=1, keepdims=True))
    last = c == pl.num_programs(1) - 1

    sn = [jnp.sum(qf * kn_ref[j:j + 1, :], axis=1, keepdims=True) + tabn_ref[:, j:j + 1] for j in range(t_new)]
    sn = [jnp.where(last, x, NEG_INF) for x in sn]
    for x in sn:
        mn = jnp.maximum(mn, x)
    safe = jnp.where(mn == NEG_INF, 0.0, mn)
    p = jnp.exp(s - safe)
    al = jnp.exp(m - safe)
    l = al * l_s[...] + jnp.sum(p, axis=1, keepdims=True)
    acc = al * acc_s[...] + jnp.dot(p.astype(BF16), vc_ref[...].astype(BF16), preferred_element_type=F32)
    for j, x in enumerate(sn):
        pj = jnp.exp(x - safe)
        l = l + pj
        acc = acc + pj * vn_ref[j:j + 1, :]
    m_s[...] = mn
    l_s[...] = l
    acc_s[...] = acc

    @pl.when(last)
    def _():
        om = (acc / l) * mask_ref[...]
        for t in range(t_new):
            o_ref[t:t + 1, :] = jnp.sum(om[t * ATT_S_ROWS:(t + 1) * ATT_S_ROWS, :], axis=0, keepdims=True)


def _attn_sample(q, cache, kv_new, rel_bias):
    bsz, t_new, _ = q.shape
    w = cache.shape[1]
    rows = t_new * ATT_S_ROWS
    vals = _distance_bias(rel_bias, MAX_WINDOW)
    vals = jnp.concatenate([vals, jnp.zeros((ATT_S_ROWS - B_HEADS, vals.shape[1]), F32)], axis=0)
    tq = np.repeat(np.arange(t_new), ATT_S_ROWS)
    hq = np.tile(np.arange(ATT_S_ROWS), t_new)
    ext = jnp.concatenate([vals[:, :MAX_WINDOW + 1], jnp.full((ATT_S_ROWS, w + t_new - MAX_WINDOW), NEG_INF, F32)], axis=1)
    ext = ext.at[B_HEADS:, :].set(0.0)
    tabc = jnp.stack([ext[:, t + 1:w + t + 1][:, ::-1] for t in range(t_new)]).reshape(rows, w)
    dist_n = tq[:, None] - np.arange(t_new)[None, :]
    dist_n = np.where(dist_n < 0, MAX_WINDOW + 1, dist_n)
    tabn = vals[jnp.asarray(hq)[:, None], jnp.asarray(dist_n)]
    mask = (hq[:, None] == (np.arange(B_WIDTH)[None, :] // B_HD)).astype(np.float32)
    qbd = (q[:, :, None, :] * jnp.asarray(mask.reshape(t_new, ATT_S_ROWS, B_WIDTH))[None]).reshape(bsz, rows, B_WIDTH)
    kc = 1024
    return pl.pallas_call(
        functools.partial(_attn_sample_kernel, t_new=t_new),
        grid=(bsz, w // kc),
        in_specs=[pl.BlockSpec((None, rows, B_WIDTH), lambda b, c: (b, 0, 0)),
                  pl.BlockSpec((None, kc, B_WIDTH), lambda b, c: (b, c, 0)),
                  pl.BlockSpec((None, kc, B_WIDTH), lambda b, c: (b, c, 1)),
                  pl.BlockSpec((None, t_new, B_WIDTH), lambda b, c: (b, 0, 0)),
                  pl.BlockSpec((None, t_new, B_WIDTH), lambda b, c: (b, 0, 1)),
                  pl.BlockSpec((rows, kc), lambda b, c: (0, c)),
                  pl.BlockSpec((rows, t_new), lambda b, c: (0, 0)),
                  pl.BlockSpec((rows, B_WIDTH), lambda b, c: (0, 0))],
        out_specs=pl.BlockSpec((None, t_new, B_WIDTH), lambda b, c: (b, 0, 0)),
        out_shape=jax.ShapeDtypeStruct((bsz, t_new, B_WIDTH), F32),
        scratch_shapes=[pltpu.VMEM((rows, 1), F32), pltpu.VMEM((rows, 1), F32), pltpu.VMEM((rows, B_WIDTH), F32)],
        compiler_params=_cp("parallel", "arbitrary"),
        name="attn_sample",
    )(qbd, cache, cache, kv_new, kv_new, tabc, tabn, jnp.asarray(mask))


def _outproj_even_kernel(*refs):
    ya_ref, yb_ref, x_ref, w_ref, g_ref, b_ref = refs[:6]
    o_ref, ot_ref = refs[-2:]
    y = jnp.dot(ya_ref[...].astype(BF16), w_ref[0:RG_WIDTH, :], preferred_element_type=F32)
    y = y + jnp.dot(yb_ref[...].astype(BF16), w_ref[RG_WIDTH:, :], preferred_element_type=F32)
    out = _ln_rows(ALPHA * x_ref[...] + y, g_ref[...], b_ref[...])
    o_ref[...] = out
    _store_row_tiles(ot_ref, out)


def _token_call(kernel_fn, name, n, x_off_rows, out_rows, out_off_rows, row_inputs, x_arr, fixed_inputs, prev):
    tm = _row_tile(n)
    nt = n // tm
    xo, oo = x_off_rows // tm, out_off_rows // tm
    tail = prev is None and out_rows > out_off_rows + n
    if tail:
        assert out_off_rows == 0 and out_rows - n <= tm
        body = kernel_fn

        def kernel_fn(*refs):
            step = pl.program_id(0)
            pl.when(step < nt)(lambda: body(*refs))

            @pl.when(step >= nt)
            def _():
                refs[-2][...] = jnp.zeros(refs[-2].shape, F32)
                refs[-1][...] = jnp.zeros(refs[-1].shape, F32)

    src = lambda i: jnp.minimum(i, nt - 1)
    in_specs = [pl.BlockSpec((tm, a.shape[1]), lambda i: (src(i), 0)) for a in row_inputs]
    in_specs.append(pl.BlockSpec((tm, D_MODEL), lambda i: (src(i) + xo, 0)))
    in_specs += [pl.BlockSpec(a.shape, lambda i, nd=a.ndim: (0,) * nd) for a in fixed_inputs]
    args = list(row_inputs) + [x_arr] + list(fixed_inputs)
    aliases = {}
    if prev is not None:
        in_specs += [pl.BlockSpec(memory_space=pl.ANY)] * 2
        aliases = {len(args): 0, len(args) + 1: 1}
        args += list(prev)
    return pl.pallas_call(
        kernel_fn,
        grid=(nt + int(tail),),
        in_specs=in_specs,
        out_specs=[pl.BlockSpec((tm, D_MODEL), lambda i: (i + oo, 0)),
                   pl.BlockSpec((tm * ROW_TILES, LANES), lambda i: (i + oo, 0))],
        out_shape=[jax.ShapeDtypeStruct((out_rows, D_MODEL), F32),
                   jax.ShapeDtypeStruct((out_rows * ROW_TILES, LANES), F32)],
        input_output_aliases=aliases,
        compiler_params=_cp("parallel"),
        name=name,
    )(*args)


def _inproj_odd_kernel(x_ref, w_ref, wdt_ref, z_ref, xbc_ref, dt_ref):
    xb = x_ref[...].astype(BF16)
    z_ref[...] = jnp.dot(xb, w_ref[:, 0:SSD_D_INNER], preferred_element_type=F32)
    xbc_ref[...] = jnp.dot(xb, w_ref[:, SSD_D_INNER:SSD_MAIN], preferred_element_type=F32)
    dt_ref[...] = jnp.dot(xb, wdt_ref[...], preferred_element_type=F32)


def _inproj_odd(x_all, row0, n, w, wdt):
    tm = _row_tile(n)
    off = row0 // tm
    widths = (SSD_D_INNER, SSD_CONV_DIM, LANES)
    return pl.pallas_call(
        _inproj_odd_kernel,
        grid=(n // tm,),
        in_specs=[pl.BlockSpec((tm, D_MODEL), lambda i: (i + off, 0)),
                  pl.BlockSpec((D_MODEL, SSD_MAIN), lambda i: (0, 0)),
                  pl.BlockSpec((D_MODEL, LANES), lambda i: (0, 0))],
        out_specs=[pl.BlockSpec((tm, c), lambda i: (i, 0)) for c in widths],
        out_shape=[jax.ShapeDtypeStruct((n, c), F32) for c in widths],
        compiler_params=_cp("parallel"),
        name="inproj_odd",
    )(x_all, w, wdt)


def _ssd_kernel(xbc_ref, dt_ref, cbuf_ref, h0_ref, cw_ref, cb_ref, dtb_ref, a_ref, dsk_ref,
                y_ref, cnew_ref, hT_ref, xx, dq, S, *, tt):
    c = pl.program_id(1)
    q = SSD_Q
    hist = CONV_W - 1
    base = 8

    @pl.when(c == 0)
    def _():
        if tt < q:
            xx[...] = jnp.zeros(xx.shape, F32)
            dq[...] = jnp.zeros(dq.shape, F32)
        xx[base - hist:base, :] = cbuf_ref[...]
        S[...] = h0_ref[...]

    @pl.when(c > 0)
    def _():
        xx[base - hist:base, :] = xx[base + tt - hist:base + tt, :]

    xx[base:base + tt, :] = xbc_ref[...]
    dq[0:tt, :] = dt_ref[...]

    @pl.when(c == pl.num_programs(1) - 1)
    def _():
        cnew_ref[...] = xx[base + tt - hist:base + tt, :]

    def conv_cols(lo, hi):
        acc = cb_ref[:, lo:hi]
        for k in range(CONV_W):
            acc = acc + cw_ref[k:k + 1, lo:hi] * _bf16_round(xx[base - hist + k:base - hist + k + q, lo:hi])
        return _silu(acc)

    row = lax.broadcasted_iota(I32, (q, q), 0)
    col = lax.broadcasted_iota(I32, (q, q), 1)
    causal = row >= col
    dt = _softplus(dq[...] + dtb_ref[...])
    if tt < q:
        dt = jnp.where(lax.broadcasted_iota(I32, (q, LANES), 0) < tt, dt, 0.0)
    da = dt * a_ref[...]
    cum = jnp.dot(causal.astype(F32), da, precision=lax.Precision.HIGHEST, preferred_element_type=F32)
    cum_t = cum.T
    dt_t = dt.T
    cum_last = cum[q - 1:q, :]
    ecum = jnp.exp(cum)
    dtde = dt * jnp.exp(cum_last - cum)
    cd = jnp.exp(cum_last)
    lo = lax.broadcasted_iota(I32, (q, LANES), 1) < SSD_P
    lo1 = lo[0:1, :]
    nt = (((1,), (1,)), ((), ()))
    pairs_per_group = SSD_HEADS // SSD_GROUPS // 2

    for g in range(SSD_GROUPS):
        bm = conv_cols(SSD_D_INNER + g * SSD_N, SSD_D_INNER + (g + 1) * SSD_N)
        cm = conv_cols(SSD_D_INNER + SSD_GN + g * SSD_N, SSD_D_INNER + SSD_GN + (g + 1) * SSD_N)
        bmb, cmb = bm.astype(BF16), cm.astype(BF16)
        cb = lax.dot_general(cmb, bmb, nt, preferred_element_type=F32)
        bm_t = bm.T.astype(BF16)
        for pp in range(pairs_per_group):
            p = g * pairs_per_group + pp
            h0, h1 = 2 * p, 2 * p + 1
            cs = slice(p * LANES, (p + 1) * LANES)
            xs = conv_cols(p * LANES, (p + 1) * LANES)
            xsb = xs.astype(BF16)
            yd = []
            for h in (h0, h1):
                seg = cum[:, h:h + 1] - cum_t[h:h + 1, :]
                m = cb * jnp.where(causal, jnp.exp(jnp.where(causal, seg, 0.0)), 0.0) * dt_t[h:h + 1, :]
                yd.append(jnp.dot(m.astype(BF16), xsb, preferred_element_type=F32))
            s_old = S[:, cs]
            yo = jnp.dot(cmb, s_old.astype(BF16), preferred_element_type=F32)
            yo = yo * jnp.where(lo, ecum[:, h0:h0 + 1], ecum[:, h1:h1 + 1])
            y = jnp.where(lo, yd[0], yd[1]) + yo + dsk_ref[:, cs] * xs
            y_ref[:, cs] = y[0:tt, :]
            wts = jnp.where(lo, dtde[:, h0:h0 + 1], dtde[:, h1:h1 + 1])
            upd = jnp.dot(bm_t, (xs * wts).astype(BF16), preferred_element_type=F32)
            S[:, cs] = s_old * jnp.where(lo1, cd[:, h0:h0 + 1], cd[:, h1:h1 + 1]) + upd

    @pl.when(c == pl.num_programs(1) - 1)
    def _():
        hT_ref[...] = S[...]


def _ssd(xbc, dt, cbuf, h0t, cw, cb, dtb, a, dsk):
    bsz, t, _ = xbc.shape
    tt = min(t, SSD_Q)
    row = lambda b, c: (b, c, 0)
    per_b = lambda b, c: (b, 0, 0)
    fixed = lambda b, c: (0, 0)
    return pl.pallas_call(
        functools.partial(_ssd_kernel, tt=tt),
        grid=(bsz, t // tt),
        in_specs=[pl.BlockSpec((None, tt, SSD_CONV_DIM), row), pl.BlockSpec((None, tt, LANES), row),
                  pl.BlockSpec((None, CONV_W - 1, SSD_CONV_DIM), per_b),
                  pl.BlockSpec((None, SSD_N, SSD_D_INNER), per_b),
                  pl.BlockSpec((CONV_W, SSD_CONV_DIM), fixed), pl.BlockSpec((1, SSD_CONV_DIM), fixed),
                  pl.BlockSpec((1, LANES), fixed), pl.BlockSpec((1, LANES), fixed),
                  pl.BlockSpec((1, SSD_D_INNER), fixed)],
        out_specs=[pl.BlockSpec((None, tt, SSD_D_INNER), row),
                   pl.BlockSpec((None, CONV_W - 1, SSD_CONV_DIM), per_b),
                   pl.BlockSpec((None, SSD_N, SSD_D_INNER), per_b)],
        out_shape=[jax.ShapeDtypeStruct((bsz, t, SSD_D_INNER), F32),
                   jax.ShapeDtypeStruct((bsz, CONV_W - 1, SSD_CONV_DIM), F32),
                   jax.ShapeDtypeStruct((bsz, SSD_N, SSD_D_INNER), F32)],
        scratch_shapes=[pltpu.VMEM((8 + SSD_Q, SSD_CONV_DIM), F32), pltpu.VMEM((SSD_Q, LANES), F32),
                        pltpu.VMEM((SSD_N, SSD_D_INNER), F32)],
        compiler_params=_cp("parallel", "arbitrary"),
        name="ssd",
    )(xbc, dt, cbuf, h0t, cw, cb, dtb, a, dsk)


def _outproj_odd_kernel(*refs):
    y_ref, z_ref, x_ref, nw_ref, w_ref, g_ref, b_ref = refs[:7]
    o_ref, ot_ref = refs[-2:]
    gw = SSD_D_INNER // SSD_GROUPS
    acc = None
    for g in range(SSD_GROUPS):
        cs = slice(g * gw, (g + 1) * gw)
        v = y_ref[:, cs] * _silu(z_ref[:, cs])
        v = v * lax.rsqrt(jnp.mean(v * v, -1, keepdims=True) + RMS_EPS) * nw_ref[:, cs]
        part = jnp.dot(v.astype(BF16), w_ref[cs, :], preferred_element_type=F32)
        acc = part if acc is None else acc + part
    out = _ln_rows(ALPHA * x_ref[...] + acc, g_ref[...], b_ref[...])
    o_ref[...] = out
    _store_row_tiles(ot_ref, out)


def _router_kernel(x_ref, w_ref, b_ref, idx_ref, gate_ref, rank_ref, cnt_ref, carry, *, tm):
    i = pl.program_id(0)

    @pl.when(i == 0)
    def _():
        carry[...] = jnp.zeros(carry.shape, F32)

    logits = jnp.dot(x_ref[...].astype(BF16), w_ref[...].astype(BF16), preferred_element_type=F32) + b_ref[...]
    lane = lax.broadcasted_iota(I32, (tm, N_EXPERTS), 1)
    work = logits
    vals, hots = [], []
    for _ in range(TOP_K):
        m = jnp.max(work, axis=1, keepdims=True)
        ik = jnp.min(jnp.where(work == m, lane, N_EXPERTS), axis=1, keepdims=True)
        hot = lane == ik
        vals.append(m)
        hots.append(hot)
        work = jnp.where(hot, NEG_INF, work)
    es = [jnp.exp(v - vals[0]) for v in vals]
    den = es[0] + es[1] + es[2] + es[3]
    hot_all = (hots[0] | hots[1] | hots[2] | hots[3])
    r_i = lax.broadcasted_iota(I32, (tm, tm), 0)
    c_i = lax.broadcasted_iota(I32, (tm, tm), 1)
    before = (r_i > c_i).astype(BF16)
    pre = jnp.dot(before, hot_all.astype(BF16), preferred_element_type=F32) + carry[...]
    lane4 = lax.broadcasted_iota(I32, (tm, TOP_K), 1)
    idx = jnp.zeros((tm, TOP_K), I32)
    gate = jnp.zeros((tm, TOP_K), F32)
    rank = jnp.zeros((tm, TOP_K), I32)
    for k in range(TOP_K):
        ik = jnp.sum(jnp.where(hots[k], lane, 0), axis=1, keepdims=True)
        rk = jnp.sum(jnp.where(hots[k], pre, 0.0), axis=1, keepdims=True).astype(I32)
        idx = jnp.where(lane4 == k, ik, idx)
        rank = jnp.where(lane4 == k, rk, rank)
        gate = jnp.where(lane4 == k, es[k] / den, gate)
    idx_ref[...] = idx
    gate_ref[...] = gate
    rank_ref[...] = rank
    carry[...] = carry[...] + jnp.sum(hot_all.astype(F32), axis=0, keepdims=True)
    cnt_ref[...] = carry[...]


def _router(x_all, w, b):
    n = x_all.shape[0]
    tm = 384 if n % 384 == 0 else 128
    out4 = lambda dt: jax.ShapeDtypeStruct((n, TOP_K), dt)
    return pl.pallas_call(
        functools.partial(_router_kernel, tm=tm),
        grid=(n // tm,),
        in_specs=[pl.BlockSpec((tm, D_MODEL), lambda i: (i, 0)),
                  pl.BlockSpec((D_MODEL, N_EXPERTS), lambda i: (0, 0)),
                  pl.BlockSpec((1, N_EXPERTS), lambda i: (0, 0))],
        out_specs=[pl.BlockSpec((tm, TOP_K), lambda i: (i, 0))] * 3 + [pl.BlockSpec((1, N_EXPERTS), lambda i: (0, 0))],
        out_shape=[out4(I32), out4(F32), out4(I32), jax.ShapeDtypeStruct((1, N_EXPERTS), F32)],
        scratch_shapes=[pltpu.VMEM((1, N_EXPERTS), F32)],
        compiler_params=_cp("arbitrary"),
        name="router",
    )(x_all, w, b)


def _tile_copy(src_hbm, row8, dst, dst_row8, sem):
    return pltpu.make_async_copy(src_hbm.at[pl.ds(pl.multiple_of(row8, ROW_TILES), ROW_TILES), :],
                                 dst.at[pl.ds(pl.multiple_of(dst_row8, ROW_TILES), ROW_TILES), :], sem)


def _ffn_kernel(te_ref, tos_ref, nv_ref, xt_hbm, w1_ref, b1_ref, w2_ref, b2_ref, ot_ref,
                xbuf, sem, w1b, w2i, w2b, *, rows):
    i = pl.program_id(0)
    nvalid = nv_ref[0]
    slot = i % 2
    half = D_FF // 2

    def gather(tile, s):
        def issue(r, _):
            _tile_copy(xt_hbm, tos_ref[tile * rows + r], xbuf.at[s], r * ROW_TILES, sem.at[s]).start()
            return 0
        lax.fori_loop(0, rows, issue, 0, unroll=16)

    @pl.when((i == 0) & (nvalid > 0))
    def _():
        gather(0, 0)

    @pl.when(i + 1 < nvalid)
    def _():
        gather(i + 1, 1 - slot)

    @pl.when((i < nvalid) & ((i == 0) | (te_ref[i] != te_ref[jnp.maximum(i - 1, 0)])))
    def _():
        w1b[...] = w1_ref[...].astype(BF16)
        for c in range(ROW_TILES):
            cs = slice(c * LANES, (c + 1) * LANES)
            w2i[c, pl.ds(0, half, stride=2), :] = w2_ref[0:half, cs]
            w2i[c, pl.ds(1, half, stride=2), :] = w2_ref[half:D_FF, cs]
            w2b[:, cs] = w2i[c].astype(BF16)

    @pl.when(i < nvalid)
    def _():
        pltpu.make_async_copy(xt_hbm.at[pl.ds(0, rows * ROW_TILES), :], xbuf.at[slot], sem.at[slot]).wait()
        xb = _load_row_tiles(xbuf.at[slot], rows).astype(BF16)
        h = jnp.dot(xb, w1b[...], preferred_element_type=F32) + b1_ref[...]
        even = (lax.broadcasted_iota(I32, (rows, LANES), 1) % 2) == 0
        parts = []
        for c in range(D_FF // LANES):
            ha = h[:, c * LANES:(c + 1) * LANES]
            hb = h[:, D_FF + c * LANES:D_FF + (c + 1) * LANES]
            glu = jnp.minimum(jnp.where(even, ha, pltpu.roll(hb, 1, 1)), SWIGLU_LIMIT)
            lin = jnp.clip(jnp.where(even, pltpu.roll(ha, LANES - 1, 1), hb), -SWIGLU_LIMIT, SWIGLU_LIMIT)
            parts.append((glu * _sigmoid(SWIGLU_ALPHA * glu) * (lin + 1.0)).astype(BF16))
        act = jnp.concatenate(parts, axis=1)
        _store_row_tiles(ot_ref, jnp.dot(act, w2b[...], preferred_element_type=F32) + b2_ref[...])

    @pl.when(i >= nvalid)
    def _():
        ot_ref[...] = jnp.zeros(ot_ref.shape, F32)


def _ffn(xt_all, tile_expert, tok_of_slot, nvalid, layer, w1, b1, w2, b2, ntiles):
    rows = MOE_ROWS
    wspec = lambda shape: pl.BlockSpec((None, None) + shape, lambda i, te, tos, nv: (layer, te[i], 0, 0))
    grid_spec = pltpu.PrefetchScalarGridSpec(
        num_scalar_prefetch=3,
        grid=(ntiles,),
        in_specs=[pl.BlockSpec(memory_space=pl.ANY),
                  wspec((D_MODEL, 2 * D_FF)), wspec((1, 2 * D_FF)),
                  wspec((D_FF, D_MODEL)), wspec((1, D_MODEL))],
        out_specs=pl.BlockSpec((rows * ROW_TILES, LANES), lambda i, te, tos, nv: (i, 0)),
        scratch_shapes=[pltpu.VMEM((2, rows * ROW_TILES, LANES), F32), pltpu.SemaphoreType.DMA((2,)),
                        pltpu.VMEM((D_MODEL, 2 * D_FF), BF16), pltpu.VMEM((ROW_TILES, D_FF, LANES), F32),
                        pltpu.VMEM((D_FF, D_MODEL), BF16)],
    )
    return pl.pallas_call(
        functools.partial(_ffn_kernel, rows=rows),
        grid_spec=grid_spec,
        out_shape=jax.ShapeDtypeStruct((ntiles * rows * ROW_TILES, LANES), F32),
        compiler_params=_cp("arbitrary"),
        name="moe_ffn",
    )(tile_expert, tok_of_slot, nvalid, xt_all, w1, b1, w2, b2)


def _combine_kernel(pos_ref, ot_hbm, gate_ref, x_ref, g_ref, b_ref, out_ref, buf, sem, *, tm):
    i = pl.program_id(0)
    n = pl.num_programs(0)
    slot = i % 2

    def gather(tile, s):
        def issue(t, _):
            for k in range(TOP_K):
                _tile_copy(ot_hbm, pos_ref[(tile * tm + t) * TOP_K + k], buf.at[s, k], t * ROW_TILES, sem.at[s]).start()
            return 0
        lax.fori_loop(0, tm, issue, 0, unroll=4)

    @pl.when(i == 0)
    def _():
        gather(0, 0)

    @pl.when(i + 1 < n)
    def _():
        gather(i + 1, 1 - slot)

    for k in range(TOP_K):
        pltpu.make_async_copy(ot_hbm.at[pl.ds(0, tm * ROW_TILES), :], buf.at[slot, k], sem.at[slot]).wait()
    f = None
    for k in range(TOP_K):
        part = gate_ref[:, k:k + 1] * _load_row_tiles(buf.at[slot, k], tm)
        f = part if f is None else f + part
    out_ref[...] = _ln_rows(ALPHA * x_ref[...] + f, g_ref[...], b_ref[...])


def _combine(pos_flat, ot_sorted, gate, x_all, g, b):
    n = x_all.shape[0]
    tm = 128
    grid_spec = pltpu.PrefetchScalarGridSpec(
        num_scalar_prefetch=1,
        grid=(n // tm,),
        in_specs=[pl.BlockSpec(memory_space=pl.ANY),
                  pl.BlockSpec((tm, TOP_K), lambda i, pos: (i, 0)),
                  pl.BlockSpec((tm, D_MODEL), lambda i, pos: (i, 0)),
                  pl.BlockSpec((1, D_MODEL), lambda i, pos: (0, 0)),
                  pl.BlockSpec((1, D_MODEL), lambda i, pos: (0, 0))],
        out_specs=pl.BlockSpec((tm, D_MODEL), lambda i, pos: (i, 0)),
        scratch_shapes=[pltpu.VMEM((2, TOP_K, tm * ROW_TILES, LANES), F32), pltpu.SemaphoreType.DMA((2,))],
    )
    return pl.pallas_call(
        functools.partial(_combine_kernel, tm=tm),
        grid_spec=grid_spec,
        out_shape=jax.ShapeDtypeStruct((n, D_MODEL), F32),
        compiler_params=_cp("arbitrary"),
        name="moe_combine",
    )(pos_flat, ot_sorted, gate, x_all, g, b)


def _moe(x_all, xt_all, layer, w_r, b_r, w1, b1, w2, b2, g, b):
    n = x_all.shape[0]
    rows = MOE_ROWS
    ntiles = -(-(n * TOP_K + N_EXPERTS * (rows - 1)) // rows)
    idx, gate, rank, cnt = _router(x_all, w_r, b_r.reshape(1, N_EXPERTS))
    counts = cnt[0].astype(I32)
    padded = ((counts + rows - 1) // rows) * rows
    ends = jnp.cumsum(padded)
    starts = ends - padded
    experts = jnp.arange(N_EXPERTS, dtype=I32)
    pos = rank + jnp.sum(jnp.where(idx[:, :, None] == experts, starts, 0), axis=-1)
    tile_starts = jnp.arange(ntiles, dtype=I32) * rows
    tile_expert = jnp.minimum(jnp.sum((tile_starts[:, None] >= ends[None, :]).astype(I32), axis=1), N_EXPERTS - 1)
    nvalid = (ends[-1:] // rows).astype(I32)
    tok = jnp.broadcast_to(jnp.arange(n, dtype=I32)[:, None] * ROW_TILES, (n, TOP_K))
    tok_of_slot = jnp.zeros((ntiles * rows,), I32).at[pos.reshape(-1)].set(tok.reshape(-1))
    ot_sorted = _ffn(xt_all, tile_expert, tok_of_slot, nvalid, layer, w1, b1[:, :, None, :], w2, b2[:, :, None, :],
                     ntiles)
    return _combine(pos.reshape(-1) * ROW_TILES, ot_sorted, gate, x_all, g.reshape(1, -1), b.reshape(1, -1))


def _block_diag(w):
    nb, bw, _ = w.shape
    eye = jnp.eye(nb, dtype=w.dtype)
    return (eye[:, None, :, None] * w[:, :, None, :]).reshape(nb * bw, nb * bw)


def _mix_even(x_arr, out_row0, n_total, bsz, t, rg_conv, rg_h, cache, p, prev):
    n = bsz * t
    xa, ga, q, kv = _inproj_even(x_arr, 0, n, p["w_in"])
    ya, conv_new, h_last = _rglru(
        xa.reshape(bsz, t, RG_WIDTH), ga.reshape(bsz, t, RG_WIDTH), rg_conv, rg_h.reshape(bsz, 1, RG_WIDTH),
        p["rg_conv_w"], p["rg_conv_b"], p["rg_wg"], p["rg_bg"], p["rg_c"])
    q3 = q.reshape(bsz, t, B_WIDTH)
    kv3 = kv.reshape(bsz, t, 2 * B_WIDTH)
    if cache is None:
        yb = _attn_prompt(q3, kv3, p["rel_bias"])
    else:
        yb = _attn_sample(q3, cache, kv3, p["rel_bias"])
    x1 = _token_call(_outproj_even_kernel, "outproj_even", n, 0, n_total, out_row0,
                     [ya.reshape(n, RG_WIDTH), yb.reshape(n, B_WIDTH)], x_arr,
                     [p["w_out"], p["ln_g0"], p["ln_b0"]], prev)
    return x1, conv_new, h_last.reshape(bsz, RG_WIDTH), kv3


def _mix_odd(x_all, row0, bsz, t, ssd_conv, ssd_h, p, prev):
    n = bsz * t
    z, xbc, dt = _inproj_odd(x_all, row0, n, p["ssd_w_main"], p["ssd_w_dt"])
    h0t = jnp.transpose(ssd_h.astype(F32), (0, 3, 1, 2)).reshape(bsz, SSD_N, SSD_D_INNER)
    y, conv_new, h_t = _ssd(xbc.reshape(bsz, t, SSD_CONV_DIM), dt.reshape(bsz, t, LANES), ssd_conv, h0t,
                            p["ssd_conv_w"], p["ssd_conv_b"], p["ssd_dtb"], p["ssd_a"], p["ssd_dsk"])
    h_last = jnp.transpose(h_t.reshape(bsz, SSD_N, SSD_HEADS, SSD_P), (0, 2, 3, 1))
    x1 = _token_call(_outproj_odd_kernel, "outproj_odd", n, row0, x_all.shape[0], row0,
                     [y.reshape(n, SSD_D_INNER), z], x_all,
                     [p["ssd_norm_w"], p["ssd_w_out"], p["ln_g2"], p["ln_b2"]], prev)
    return x1, conv_new, h_last


def _mixer_params(rel_bias, w_in_mix, rg_conv_w, rg_conv_b, rg_w_a, rg_b_a, rg_w_i, rg_b_i, rg_lambda, w_out_mix,
                  ssd_w_in, ssd_conv_w, ssd_conv_b, ssd_dt_bias, ssd_a_log, ssd_d, ssd_norm_w, ssd_w_out, ln_g, ln_b):
    row2 = lambda v: v.reshape(1, -1).astype(F32)
    pad_heads = lambda v: jnp.pad(v.astype(F32), (0, LANES - SSD_HEADS)).reshape(1, LANES)
    return dict(
        rel_bias=rel_bias,
        w_in=w_in_mix[0].astype(BF16),
        rg_conv_w=rg_conv_w[0].astype(F32), rg_conv_b=row2(rg_conv_b[0]),
        rg_wg=jnp.concatenate([_block_diag(rg_w_a[0]), _block_diag(rg_w_i[0])], axis=1).astype(BF16),
        rg_bg=jnp.concatenate([rg_b_a[0].reshape(1, -1), rg_b_i[0].reshape(1, -1)], axis=1).astype(F32),
        rg_c=row2(-RG_C * jax.nn.softplus(-rg_lambda[0].astype(F32))),
        w_out=w_out_mix[0].astype(BF16),
        ln_g0=row2(ln_g[0, 0]), ln_b0=row2(ln_b[0, 0]),
        ssd_w_main=ssd_w_in[0][:, :SSD_MAIN].astype(BF16),
        ssd_w_dt=jnp.pad(ssd_w_in[0][:, SSD_MAIN:], ((0, 0), (0, LANES - SSD_HEADS))).astype(BF16),
        ssd_conv_w=ssd_conv_w[0].astype(F32), ssd_conv_b=row2(ssd_conv_b[0]),
        ssd_dtb=pad_heads(ssd_dt_bias[0]), ssd_a=pad_heads(-jnp.exp(ssd_a_log[0].astype(F32))),
        ssd_dsk=row2(jnp.repeat(ssd_d[0].astype(F32), SSD_P)),
        ssd_norm_w=row2(ssd_norm_w[0]), ssd_w_out=ssd_w_out[0].astype(BF16),
        ln_g2=row2(ln_g[1, 0]), ln_b2=row2(ln_b[1, 0]),
    )


def kernel(x_prompt, x_sample, state_rglru_conv, state_rglru_h, cache_swa_kv, state_ssd_conv, state_ssd_h,
           rel_bias, w_in_mix, rg_conv_w, rg_conv_b, rg_w_a, rg_b_a, rg_w_i, rg_b_i, rg_lambda, w_out_mix,
           ssd_w_in, ssd_conv_w, ssd_conv_b, ssd_dt_bias, ssd_a_log, ssd_d, ssd_norm_w, ssd_w_out,
           ln_g, ln_b, router_w, router_b, exp_w1, exp_b1, exp_w2, exp_b2):
    bp, tp, _ = x_prompt.shape
    bs, ts, _ = x_sample.shape
    n_p, n_s = bp * tp, bs * ts
    n_total = n_p + n_s
    p = _mixer_params(rel_bias, w_in_mix, rg_conv_w, rg_conv_b, rg_w_a, rg_b_a, rg_w_i, rg_b_i, rg_lambda, w_out_mix,
                      ssd_w_in, ssd_conv_w, ssd_conv_b, ssd_dt_bias, ssd_a_log, ssd_d, ssd_norm_w, ssd_w_out, ln_g, ln_b)

    xp = x_prompt.reshape(n_p, D_MODEL)
    xs = x_sample.reshape(n_s, D_MODEL)
    zeros = lambda *s: jnp.zeros(s, F32)
    x1, p_rg_conv, p_rg_h, p_kv = _mix_even(xp, 0, n_total, bp, tp, zeros(bp, CONV_W - 1, RG_WIDTH),
                                            zeros(bp, RG_WIDTH), None, p, None)
    cache = cache_swa_kv[0].reshape(bs, cache_swa_kv.shape[2], 2 * B_WIDTH)
    x1, s_rg_conv, s_rg_h, s_kv = _mix_even(xs, n_p, n_total, bs, ts, state_rglru_conv[0], state_rglru_h[0],
                                            cache, p, x1)
    x2 = _moe(x1[0], x1[1], 0, router_w[0].astype(F32), router_b[0].astype(F32), exp_w1, exp_b1, exp_w2, exp_b2,
              ln_g[0, 1], ln_b[0, 1])

    x3, p_ssd_conv, p_ssd_h = _mix_odd(x2, 0, bp, tp, zeros(bp, CONV_W - 1, SSD_CONV_DIM),
                                       zeros(bp, SSD_HEADS, SSD_P, SSD_N), p, None)
    x3, s_ssd_conv, s_ssd_h = _mix_odd(x2, n_p, bs, ts, state_ssd_conv[0], state_ssd_h[0], p, x3)
    x4 = _moe(x3[0], x3[1], 1, router_w[1].astype(F32), router_b[1].astype(F32), exp_w1, exp_b1, exp_w2, exp_b2,
              ln_g[1, 1], ln_b[1, 1])

    kv_shape = lambda b, t: (1, b, t, 2, B_HEADS, B_HD)
    return (x4[:n_p].reshape(bp, tp, D_MODEL), x4[n_p:].reshape(bs, ts, D_MODEL),
            p_rg_conv[None], p_rg_h[None], p_kv[:, -min(MAX_WINDOW, tp):].reshape(kv_shape(bp, min(MAX_WINDOW, tp))),
            p_ssd_conv[None], p_ssd_h[None],
            s_rg_conv[None], s_rg_h[None], s_kv.reshape(kv_shape(bs, ts)),
            s_ssd_conv[None], s_ssd_h[None])
```

```python
import functools
import math

import numpy as np
import jax
import jax.numpy as jnp
from jax import lax
from jax.experimental import pallas as pl
from jax.experimental.pallas import tpu as pltpu

F32 = jnp.float32
BF16 = jnp.bfloat16
I32 = jnp.int32

D_MODEL = 1024
DEPTH = 2
ALPHA = (2 * DEPTH) ** 0.25
LN_EPS = 1e-5
RMS_EPS = 1e-5
CONV_W = 4
RG_BLOCKS = 12
RG_BW = 64
RG_WIDTH = RG_BLOCKS * RG_BW
RG_C = 8.0
B_HEADS = 12
B_HD = 64
B_WIDTH = B_HEADS * B_HD
PATTERNS = ((128, 1), (512, 4), (2048, 16))
MAX_WINDOW = 2048
N_BUCKETS = 32
MAX_DISTANCE = MAX_WINDOW
EVEN_IN = 2 * RG_WIDTH + 3 * B_WIDTH
SSD_D_INNER = 2 * D_MODEL
SSD_P = 64
SSD_HEADS = SSD_D_INNER // SSD_P
SSD_N = 128
SSD_GROUPS = 4
SSD_GN = SSD_GROUPS * SSD_N
SSD_CONV_DIM = SSD_D_INNER + 2 * SSD_GN
SSD_MAIN = SSD_D_INNER + SSD_CONV_DIM
N_EXPERTS = 32
TOP_K = 4
D_FF = D_MODEL
SWIGLU_LIMIT = 7.0
SWIGLU_ALPHA = 1.702

LANES = 128
VMEM_LIMIT = 56 * 1024 * 1024

SSD_Q = 128
ATT_BLK = 256
ATT_CHUNK = 64
MOE_ROWS = 256
NEG_INF = float("-inf")


ROW_TILES = D_MODEL // LANES


def _cp(*sem):
    return pltpu.CompilerParams(dimension_semantics=sem, vmem_limit_bytes=VMEM_LIMIT)


def _store_row_tiles(ref, val):
    rows = val.shape[0]
    for s in range(ROW_TILES):
        ref[pl.ds(s, rows, stride=ROW_TILES), :] = val[:, s * LANES:(s + 1) * LANES]


def _load_row_tiles(ref, rows):
    return jnp.concatenate([ref[pl.ds(s, rows, stride=ROW_TILES), :] for s in range(ROW_TILES)], axis=1)


def _ln_rows(v, g, b):
    mu = jnp.mean(v, -1, keepdims=True)
    xc = v - mu
    var = jnp.mean(xc * xc, -1, keepdims=True)
    return xc * lax.rsqrt(var + LN_EPS) * g + b


def _bf16_round(x):
    return x.astype(BF16).astype(F32)


def _sigmoid(x):
    return 1.0 / (1.0 + jnp.exp(-x))


def _silu(x):
    return x * _sigmoid(x)


def _softplus(x):
    return jnp.maximum(x, 0.0) + jnp.log1p(jnp.exp(-jnp.abs(x)))


def _gelu_tanh(x):
    return 0.5 * x * (1.0 + jnp.tanh(math.sqrt(2.0 / math.pi) * (x + 0.044715 * (x * x * x))))


def _row_tile(n):
    return min(n, 256)


def _inproj_even_kernel(x_ref, w_ref, xa_ref, ga_ref, q_ref, kv_ref):
    xb = x_ref[...].astype(BF16)
    c0, c1, c2 = RG_WIDTH, 2 * RG_WIDTH, 2 * RG_WIDTH + B_WIDTH
    xa_ref[...] = jnp.dot(xb, w_ref[:, 0:c0], preferred_element_type=F32)
    ga_ref[...] = jnp.dot(xb, w_ref[:, c0:c1], preferred_element_type=F32)
    q_ref[...] = jnp.dot(xb, w_ref[:, c1:c2], preferred_element_type=F32)
    kv_ref[...] = jnp.dot(xb, w_ref[:, c2:EVEN_IN], preferred_element_type=F32)


def _inproj_even(x_all, row0, n, w):
    tm = _row_tile(n)
    off = row0 // tm
    widths = (RG_WIDTH, RG_WIDTH, B_WIDTH, 2 * B_WIDTH)
    return pl.pallas_call(
        _inproj_even_kernel,
        grid=(n // tm,),
        in_specs=[pl.BlockSpec((tm, D_MODEL), lambda i: (i + off, 0)),
                  pl.BlockSpec((D_MODEL, EVEN_IN), lambda i: (0, 0))],
        out_specs=[pl.BlockSpec((tm, c), lambda i: (i, 0)) for c in widths],
        out_shape=[jax.ShapeDtypeStruct((n, c), F32) for c in widths],
        compiler_params=_cp("parallel"),
        name="inproj_even",
    )(x_all, w)


def _rglru_kernel(xa_ref, ga_ref, cbuf_ref, h0_ref, cw_ref, cb_ref, wg_ref, bg_ref, c_ref,
                  ya_ref, cnew_ref, hlast_ref, xx, a_s, u_s, hc, *, tt):
    j = pl.program_id(1)
    hist = CONV_W - 1
    base = 8

    @pl.when(j == 0)
    def _():
        xx[base - hist:base, :] = cbuf_ref[...]
        hc[...] = h0_ref[...]

    @pl.when(j > 0)
    def _():
        xx[base - hist:base, :] = xx[base + tt - hist:base + tt, :]

    xx[base:base + tt, :] = xa_ref[...]
    conv = cb_ref[...]
    for k in range(CONV_W):
        conv = conv + cw_ref[k:k + 1, :] * _bf16_round(xx[base - hist + k:base - hist + k + tt, :])
    gates = jnp.dot(conv.astype(BF16), wg_ref[...], preferred_element_type=F32) + bg_ref[...]
    r = _sigmoid(gates[:, :RG_WIDTH])
    ig = _sigmoid(gates[:, RG_WIDTH:])
    log_a = c_ref[...] * r
    a = jnp.exp(log_a)
    a_s[...] = a
    u_s[...] = jnp.sqrt(-jnp.tanh(log_a) * (a * a + 1.0)) * (ig * conv)

    def step(t, h):
        h = a_s[pl.ds(t, 1), :] * h + u_s[pl.ds(t, 1), :]
        u_s[pl.ds(t, 1), :] = h
        return h

    h = lax.fori_loop(0, tt, step, hc[...], unroll=min(tt, 8))
    hc[...] = h
    ya_ref[...] = _gelu_tanh(ga_ref[...]) * u_s[...]

    @pl.when(j == pl.num_programs(1) - 1)
    def _():
        hlast_ref[...] = h
        cnew_ref[...] = xx[base + tt - hist:base + tt, :]


def _rglru(xa, ga, cbuf, h0, cw, cb, wg, bg, cvec):
    bsz, t, c = xa.shape
    tt = min(t, 256)
    row = lambda b, j: (b, j, 0)
    per_b = lambda b, j: (b, 0, 0)
    fixed = lambda b, j: (0, 0)
    return pl.pallas_call(
        functools.partial(_rglru_kernel, tt=tt),
        grid=(bsz, t // tt),
        in_specs=[pl.BlockSpec((None, tt, c), row), pl.BlockSpec((None, tt, c), row),
                  pl.BlockSpec((None, CONV_W - 1, c), per_b), pl.BlockSpec((None, 1, c), per_b),
                  pl.BlockSpec((CONV_W, c), fixed), pl.BlockSpec((1, c), fixed),
                  pl.BlockSpec((c, 2 * c), fixed), pl.BlockSpec((1, 2 * c), fixed),
                  pl.BlockSpec((1, c), fixed)],
        out_specs=[pl.BlockSpec((None, tt, c), row), pl.BlockSpec((None, CONV_W - 1, c), per_b),
                   pl.BlockSpec((None, 1, c), per_b)],
        out_shape=[jax.ShapeDtypeStruct((bsz, t, c), F32),
                   jax.ShapeDtypeStruct((bsz, CONV_W - 1, c), F32),
                   jax.ShapeDtypeStruct((bsz, 1, c), F32)],
        scratch_shapes=[pltpu.VMEM((8 + tt, c), F32), pltpu.VMEM((tt, c), F32),
                        pltpu.VMEM((tt, c), F32), pltpu.VMEM((1, c), F32)],
        compiler_params=_cp("parallel", "arbitrary"),
        name="rglru",
    )(xa, ga, cbuf, h0, cw, cb, wg, bg, cvec)


def _t5_bucket(dist):
    max_exact = N_BUCKETS // 2
    safe = np.maximum(dist, 1)
    large = max_exact + (np.log(safe / max_exact) / np.log(MAX_DISTANCE / max_exact)
                         * (N_BUCKETS - max_exact)).astype(np.int32)
    return np.where(dist < max_exact, dist, np.minimum(large, N_BUCKETS - 1)).astype(np.int32)


def _distance_bias(rel_bias, max_d):
    d = np.arange(max_d + 1)
    mult = np.zeros(max_d + 1, np.float64)
    for w, dil in PATTERNS:
        mult += ((d % dil == 0) & (d <= w)).astype(np.float64)
    logm = np.where(mult > 0, np.log(np.maximum(mult, 1.0)), -np.inf).astype(np.float32)
    vals = rel_bias.astype(F32)[_t5_bucket(d)].T + jnp.asarray(logm)[None, :]
    return jnp.concatenate([vals, jnp.full((vals.shape[0], 1), NEG_INF, F32)], axis=1)


def _attn_prompt_kernel(q_ref, k_ref, v_ref, rv_ref, o_ref, tab_ref, k_s, v_s, *, blk, nq):
    qi = pl.program_id(2)

    @pl.when((pl.program_id(1) == 0) & (qi == 0))
    def _():
        for h in range(2):
            for d in range(nq):
                win = rv_ref[h:h + 1, (nq - 1 - d) * blk:(nq + 1 - d) * blk]
                rolled = pltpu.roll(jnp.broadcast_to(win, (blk, 2 * blk)), 0, 1, stride=1, stride_axis=0)
                tab_ref[h, d] = rolled[:, blk:]

    lo = lax.broadcasted_iota(I32, (blk, LANES), 1) < B_HD

    @pl.when(qi == 0)
    def _():
        k_s[...] = k_ref[...].astype(BF16)
        v_s[...] = v_ref[...].astype(BF16)

    q2 = q_ref[...] * (B_HD ** -0.5)
    qh = (jnp.where(lo, q2, 0.0).astype(BF16), jnp.where(lo, 0.0, q2).astype(BF16))
    nt = (((1,), (1,)), ((), ()))
    ch = ATT_CHUNK

    def scores(d):
        k = k_s[pl.ds(pl.multiple_of((qi - d) * blk, blk), blk), :]
        return tuple(lax.dot_general(qh[h], k, nt, preferred_element_type=F32) for h in range(2))

    def body(d, carry):
        m0, l0, m1, l1, acc, s0, s1 = carry
        s_next = scores(jnp.minimum(d + 1, qi))
        v = v_s[pl.ds(pl.multiple_of((qi - d) * blk, blk), blk), :]
        new = []
        for h, (m, l, s) in enumerate(((m0, l0, s0), (m1, l1, s1))):
            mns, lns, als, ps = [], [], [], []
            for c in range(blk // ch):
                rs = slice(c * ch, (c + 1) * ch)
                sc = s[rs] + tab_ref[h, d, rs, :]
                mn = jnp.maximum(m[rs], jnp.broadcast_to(jnp.max(sc, axis=1, keepdims=True), (ch, LANES)))
                p = jnp.concatenate([jnp.exp(sc[:, j * LANES:(j + 1) * LANES] - mn) for j in range(blk // LANES)], axis=1)
                al = jnp.exp(m[rs] - mn)
                mns.append(mn)
                als.append(al)
                lns.append(al * l[rs] + jnp.broadcast_to(jnp.sum(p, axis=1, keepdims=True), (ch, LANES)))
                ps.append(p.astype(BF16))
            cat = lambda xs: jnp.concatenate(xs, axis=0)
            new.append((cat(mns), cat(lns), cat(als), jnp.dot(cat(ps), v, preferred_element_type=F32)))
        (m0, l0, a0, pv0), (m1, l1, a1, pv1) = new
        acc = jnp.where(lo, a0, a1) * acc + jnp.where(lo, pv0, pv1)
        return (m0, l0, m1, l1, acc) + s_next

    minf = jnp.full((blk, LANES), NEG_INF, F32)
    zero = jnp.zeros((blk, LANES), F32)
    out = lax.fori_loop(0, qi + 1, body, (minf, zero, minf, zero, zero) + scores(0))
    m0, l0, m1, l1, acc = out[:5]
    o_ref[...] = acc / jnp.where(lo, l0, l1)


def _attn_prompt(q, kv, rel_bias):
    bsz, t, _ = q.shape
    blk = min(ATT_BLK, t)
    nq = t // blk
    npair = B_HEADS // 2
    vals = _distance_bias(rel_bias, t)[:, :t + 1]
    rv = jnp.concatenate([jnp.full((B_HEADS, blk - 1), NEG_INF, F32), vals], axis=1)[:, ::-1]
    rv = rv.reshape(npair, 2, t + blk)
    return pl.pallas_call(
        functools.partial(_attn_prompt_kernel, blk=blk, nq=nq),
        grid=(npair, bsz, nq),
        in_specs=[pl.BlockSpec((None, blk, LANES), lambda p, b, i: (b, i, p)),
                  pl.BlockSpec((None, t, LANES), lambda p, b, i: (b, 0, p)),
                  pl.BlockSpec((None, t, LANES), lambda p, b, i: (b, 0, npair + p)),
                  pl.BlockSpec((None, 2, t + blk), lambda p, b, i: (p, 0, 0))],
        out_specs=pl.BlockSpec((None, blk, LANES), lambda p, b, i: (b, i, p)),
        out_shape=jax.ShapeDtypeStruct((bsz, t, B_WIDTH), F32),
        scratch_shapes=[pltpu.VMEM((2, nq, blk, blk), F32), pltpu.VMEM((t, LANES), BF16),
                        pltpu.VMEM((t, LANES), BF16)],
        compiler_params=_cp("arbitrary", "arbitrary", "arbitrary"),
        name="attn_prompt",
    )(q, kv, kv, rv)


ATT_S_ROWS = 16


def _attn_sample_kernel(qbd_ref, kc_ref, vc_ref, kn_ref, vn_ref, tabc_ref, tabn_ref, mask_ref, o_ref,
                        m_s, l_s, acc_s, *, t_new):
    c = pl.program_id(1)
    rows = t_new * ATT_S_ROWS
    nt = (((1,), (1,)), ((), ()))

    @pl.when(c == 0)
    def _():
        m_s[...] = jnp.full((rows, 1), NEG_INF, F32)
        l_s[...] = jnp.zeros((rows, 1), F32)
        acc_s[...] = jnp.zeros((rows, B_WIDTH), F32)

    qf = qbd_ref[...] * (B_HD ** -0.5)
    qb = qf.astype(BF16)
    s = lax.dot_general(qb, kc_ref[...].astype(BF16), nt, preferred_element_type=F32) + tabc_ref[...]
    m = m_s[...]
    mn = jnp.maximum(m, jnp.max(s, axis=1, keepdims=True))
    last = c == pl.num_programs(1) - 1

    sn = [jnp.sum(qf * kn_ref[j:j + 1, :], axis=1, keepdims=True) + tabn_ref[:, j:j + 1] for j in range(t_new)]
    sn = [jnp.where(last, x, NEG_INF) for x in sn]
    for x in sn:
        mn = jnp.maximum(mn, x)
    safe = jnp.where(mn == NEG_INF, 0.0, mn)
    p = jnp.exp(s - safe)
    al = jnp.exp(m - safe)
    l = al * l_s[...] + jnp.sum(p, axis=1, keepdims=True)
    acc = al * acc_s[...] + jnp.dot(p.astype(BF16), vc_ref[...].astype(BF16), preferred_element_type=F32)
    for j, x in enumerate(sn):
        pj = jnp.exp(x - safe)
        l = l + pj
        acc = acc + pj * vn_ref[j:j + 1, :]
    m_s[...] = mn
    l_s[...] = l
    acc_s[...] = acc

    @pl.when(last)
    def _():
        om = (acc / l) * mask_ref[...]
        for t in range(t_new):
            o_ref[t:t + 1, :] = jnp.sum(om[t * ATT_S_ROWS:(t + 1) * ATT_S_ROWS, :], axis=0, keepdims=True)


def _attn_sample(q, cache, kv_new, rel_bias):
    bsz, t_new, _ = q.shape
    w = cache.shape[1]
    rows = t_new * ATT_S_ROWS
    vals = _distance_bias(rel_bias, MAX_WINDOW)
    vals = jnp.concatenate([vals, jnp.zeros((ATT_S_ROWS - B_HEADS, vals.shape[1]), F32)], axis=0)
    tq = np.repeat(np.arange(t_new), ATT_S_ROWS)
    hq = np.tile(np.arange(ATT_S_ROWS), t_new)
    ext = jnp.concatenate([vals[:, :MAX_WINDOW + 1], jnp.full((ATT_S_ROWS, w + t_new - MAX_WINDOW), NEG_INF, F32)], axis=1)
    ext = ext.at[B_HEADS:, :].set(0.0)
    tabc = jnp.stack([ext[:, t + 1:w + t + 1][:, ::-1] for t in range(t_new)]).reshape(rows, w)
    dist_n = tq[:, None] - np.arange(t_new)[None, :]
    dist_n = np.where(dist_n < 0, MAX_WINDOW + 1, dist_n)
    tabn = vals[jnp.asarray(hq)[:, None], jnp.asarray(dist_n)]
    mask = (hq[:, None] == (np.arange(B_WIDTH)[None, :] // B_HD)).astype(np.float32)
    qbd = (q[:, :, None, :] * jnp.asarray(mask.reshape(t_new, ATT_S_ROWS, B_WIDTH))[None]).reshape(bsz, rows, B_WIDTH)
    kc = 1024
    return pl.pallas_call(
        functools.partial(_attn_sample_kernel, t_new=t_new),
        grid=(bsz, w // kc),
        in_specs=[pl.BlockSpec((None, rows, B_WIDTH), lambda b, c: (b, 0, 0)),
                  pl.BlockSpec((None, kc, B_WIDTH), lambda b, c: (b, c, 0)),
                  pl.BlockSpec((None, kc, B_WIDTH), lambda b, c: (b, c, 1)),
                  pl.BlockSpec((None, t_new, B_WIDTH), lambda b, c: (b, 0, 0)),
                  pl.BlockSpec((None, t_new, B_WIDTH), lambda b, c: (b, 0, 1)),
                  pl.BlockSpec((rows, kc), lambda b, c: (0, c)),
                  pl.BlockSpec((rows, t_new), lambda b, c: (0, 0)),
                  pl.BlockSpec((rows, B_WIDTH), lambda b, c: (0, 0))],
        out_specs=pl.BlockSpec((None, t_new, B_WIDTH), lambda b, c: (b, 0, 0)),
        out_shape=jax.ShapeDtypeStruct((bsz, t_new, B_WIDTH), F32),
        scratch_shapes=[pltpu.VMEM((rows, 1), F32), pltpu.VMEM((rows, 1), F32), pltpu.VMEM((rows, B_WIDTH), F32)],
        compiler_params=_cp("parallel", "arbitrary"),
        name="attn_sample",
    )(qbd, cache, cache, kv_new, kv_new, tabc, tabn, jnp.asarray(mask))


def _outproj_even_kernel(*refs):
    ya_ref, yb_ref, x_ref, w_ref, g_ref, b_ref = refs[:6]
    o_ref, ot_ref = refs[-2:]
    y = jnp.dot(ya_ref[...].astype(BF16), w_ref[0:RG_WIDTH, :], preferred_element_type=F32)
    y = y + jnp.dot(yb_ref[...].astype(BF16), w_ref[RG_WIDTH:, :], preferred_element_type=F32)
    out = _ln_rows(ALPHA * x_ref[...] + y, g_ref[...], b_ref[...])
    o_ref[...] = out
    _store_row_tiles(ot_ref, out)


def _token_call(kernel_fn, name, n, x_off_rows, out_rows, out_off_rows, row_inputs, x_arr, fixed_inputs, prev):
    tm = _row_tile(n)
    nt = n // tm
    xo, oo = x_off_rows // tm, out_off_rows // tm
    tail = prev is None and out_rows > out_off_rows + n
    if tail:
        assert out_off_rows == 0 and out_rows - n <= tm
        body = kernel_fn

        def kernel_fn(*refs):
            step = pl.program_id(0)
            pl.when(step < nt)(lambda: body(*refs))

            @pl.when(step >= nt)
            def _():
                refs[-2][...] = jnp.zeros(refs[-2].shape, F32)
                refs[-1][...] = jnp.zeros(refs[-1].shape, F32)

    src = lambda i: jnp.minimum(i, nt - 1)
    in_specs = [pl.BlockSpec((tm, a.shape[1]), lambda i: (src(i), 0)) for a in row_inputs]
    in_specs.append(pl.BlockSpec((tm, D_MODEL), lambda i: (src(i) + xo, 0)))
    in_specs += [pl.BlockSpec(a.shape, lambda i, nd=a.ndim: (0,) * nd) for a in fixed_inputs]
    args = list(row_inputs) + [x_arr] + list(fixed_inputs)
    aliases = {}
    if prev is not None:
        in_specs += [pl.BlockSpec(memory_space=pl.ANY)] * 2
        aliases = {len(args): 0, len(args) + 1: 1}
        args += list(prev)
    return pl.pallas_call(
        kernel_fn,
        grid=(nt + int(tail),),
        in_specs=in_specs,
        out_specs=[pl.BlockSpec((tm, D_MODEL), lambda i: (i + oo, 0)),
                   pl.BlockSpec((tm * ROW_TILES, LANES), lambda i: (i + oo, 0))],
        out_shape=[jax.ShapeDtypeStruct((out_rows, D_MODEL), F32),
                   jax.ShapeDtypeStruct((out_rows * ROW_TILES, LANES), F32)],
        input_output_aliases=aliases,
        compiler_params=_cp("parallel"),
        name=name,
    )(*args)


def _inproj_odd_kernel(x_ref, w_ref, wdt_ref, z_ref, xbc_ref, dt_ref):
    xb = x_ref[...].astype(BF16)
    z_ref[...] = jnp.dot(xb, w_ref[:, 0:SSD_D_INNER], preferred_element_type=F32)
    xbc_ref[...] = jnp.dot(xb, w_ref[:, SSD_D_INNER:SSD_MAIN], preferred_element_type=F32)
    dt_ref[...] = jnp.dot(xb, wdt_ref[...], preferred_element_type=F32)


def _inproj_odd(x_all, row0, n, w, wdt):
    tm = _row_tile(n)
    off = row0 // tm
    widths = (SSD_D_INNER, SSD_CONV_DIM, LANES)
    return pl.pallas_call(
        _inproj_odd_kernel,
        grid=(n // tm,),
        in_specs=[pl.BlockSpec((tm, D_MODEL), lambda i: (i + off, 0)),
                  pl.BlockSpec((D_MODEL, SSD_MAIN), lambda i: (0, 0)),
                  pl.BlockSpec((D_MODEL, LANES), lambda i: (0, 0))],
        out_specs=[pl.BlockSpec((tm, c), lambda i: (i, 0)) for c in widths],
        out_shape=[jax.ShapeDtypeStruct((n, c), F32) for c in widths],
        compiler_params=_cp("parallel"),
        name="inproj_odd",
    )(x_all, w, wdt)


def _ssd_kernel(xbc_ref, dt_ref, cbuf_ref, h0_ref, cw_ref, cb_ref, dtb_ref, a_ref, dsk_ref,
                y_ref, cnew_ref, hT_ref, xx, dq, S, *, tt):
    c = pl.program_id(1)
    q = SSD_Q
    hist = CONV_W - 1
    base = 8

    @pl.when(c == 0)
    def _():
        if tt < q:
            xx[...] = jnp.zeros(xx.shape, F32)
            dq[...] = jnp.zeros(dq.shape, F32)
        xx[base - hist:base, :] = cbuf_ref[...]
        S[...] = h0_ref[...]

    @pl.when(c > 0)
    def _():
        xx[base - hist:base, :] = xx[base + tt - hist:base + tt, :]

    xx[base:base + tt, :] = xbc_ref[...]
    dq[0:tt, :] = dt_ref[...]

    @pl.when(c == pl.num_programs(1) - 1)
    def _():
        cnew_ref[...] = xx[base + tt - hist:base + tt, :]

    def conv_cols(lo, hi):
        acc = cb_ref[:, lo:hi]
        for k in range(CONV_W):
            acc = acc + cw_ref[k:k + 1, lo:hi] * _bf16_round(xx[base - hist + k:base - hist + k + q, lo:hi])
        return _silu(acc)

    row = lax.broadcasted_iota(I32, (q, q), 0)
    col = lax.broadcasted_iota(I32, (q, q), 1)
    causal = row >= col
    dt = _softplus(dq[...] + dtb_ref[...])
    if tt < q:
        dt = jnp.where(lax.broadcasted_iota(I32, (q, LANES), 0) < tt, dt, 0.0)
    da = dt * a_ref[...]
    cum = jnp.dot(causal.astype(F32), da, precision=lax.Precision.HIGHEST, preferred_element_type=F32)
    cum_t = cum.T
    dt_t = dt.T
    cum_last = cum[q - 1:q, :]
    ecum = jnp.exp(cum)
    dtde = dt * jnp.exp(cum_last - cum)
    cd = jnp.exp(cum_last)
    lo = lax.broadcasted_iota(I32, (q, LANES), 1) < SSD_P
    lo1 = lo[0:1, :]
    nt = (((1,), (1,)), ((), ()))
    pairs_per_group = SSD_HEADS // SSD_GROUPS // 2

    for g in range(SSD_GROUPS):
        bm = conv_cols(SSD_D_INNER + g * SSD_N, SSD_D_INNER + (g + 1) * SSD_N)
        cm = conv_cols(SSD_D_INNER + SSD_GN + g * SSD_N, SSD_D_INNER + SSD_GN + (g + 1) * SSD_N)
        bmb, cmb = bm.astype(BF16), cm.astype(BF16)
        cb = lax.dot_general(cmb, bmb, nt, preferred_element_type=F32)
        bm_t = bm.T.astype(BF16)
        for pp in range(pairs_per_group):
            p = g * pairs_per_group + pp
            h0, h1 = 2 * p, 2 * p + 1
            cs = slice(p * LANES, (p + 1) * LANES)
            xs = conv_cols(p * LANES, (p + 1) * LANES)
            xsb = xs.astype(BF16)
            yd = []
            for h in (h0, h1):
                seg = cum[:, h:h + 1] - cum_t[h:h + 1, :]
                m = cb * jnp.where(causal, jnp.exp(jnp.where(causal, seg, 0.0)), 0.0) * dt_t[h:h + 1, :]
                yd.append(jnp.dot(m.astype(BF16), xsb, preferred_element_type=F32))
            s_old = S[:, cs]
            yo = jnp.dot(cmb, s_old.astype(BF16), preferred_element_type=F32)
            yo = yo * jnp.where(lo, ecum[:, h0:h0 + 1], ecum[:, h1:h1 + 1])
            y = jnp.where(lo, yd[0], yd[1]) + yo + dsk_ref[:, cs] * xs
            y_ref[:, cs] = y[0:tt, :]
            wts = jnp.where(lo, dtde[:, h0:h0 + 1], dtde[:, h1:h1 + 1])
            upd = jnp.dot(bm_t, (xs * wts).astype(BF16), preferred_element_type=F32)
            S[:, cs] = s_old * jnp.where(lo1, cd[:, h0:h0 + 1], cd[:, h1:h1 + 1]) + upd

    @pl.when(c == pl.num_programs(1) - 1)
    def _():
        hT_ref[...] = S[...]


def _ssd(xbc, dt, cbuf, h0t, cw, cb, dtb, a, dsk):
    bsz, t, _ = xbc.shape
    tt = min(t, SSD_Q)
    row = lambda b, c: (b, c, 0)
    per_b = lambda b, c: (b, 0, 0)
    fixed = lambda b, c: (0, 0)
    return pl.pallas_call(
        functools.partial(_ssd_kernel, tt=tt),
        grid=(bsz, t // tt),
        in_specs=[pl.BlockSpec((None, tt, SSD_CONV_DIM), row), pl.BlockSpec((None, tt, LANES), row),
                  pl.BlockSpec((None, CONV_W - 1, SSD_CONV_DIM), per_b),
                  pl.BlockSpec((None, SSD_N, SSD_D_INNER), per_b),
                  pl.BlockSpec((CONV_W, SSD_CONV_DIM), fixed), pl.BlockSpec((1, SSD_CONV_DIM), fixed),
                  pl.BlockSpec((1, LANES), fixed), pl.BlockSpec((1, LANES), fixed),
                  pl.BlockSpec((1, SSD_D_INNER), fixed)],
        out_specs=[pl.BlockSpec((None, tt, SSD_D_INNER), row),
                   pl.BlockSpec((None, CONV_W - 1, SSD_CONV_DIM), per_b),
                   pl.BlockSpec((None, SSD_N, SSD_D_INNER), per_b)],
        out_shape=[jax.ShapeDtypeStruct((bsz, t, SSD_D_INNER), F32),
                   jax.ShapeDtypeStruct((bsz, CONV_W - 1, SSD_CONV_DIM), F32),
                   jax.ShapeDtypeStruct((bsz, SSD_N, SSD_D_INNER), F32)],
        scratch_shapes=[pltpu.VMEM((8 + SSD_Q, SSD_CONV_DIM), F32), pltpu.VMEM((SSD_Q, LANES), F32),
                        pltpu.VMEM((SSD_N, SSD_D_INNER), F32)],
        compiler_params=_cp("parallel", "arbitrary"),
        name="ssd",
    )(xbc, dt, cbuf, h0t, cw, cb, dtb, a, dsk)


def _outproj_odd_kernel(*refs):
    y_ref, z_ref, x_ref, nw_ref, w_ref, g_ref, b_ref = refs[:7]
    o_ref, ot_ref = refs[-2:]
    gw = SSD_D_INNER // SSD_GROUPS
    acc = None
    for g in range(SSD_GROUPS):
        cs = slice(g * gw, (g + 1) * gw)
        v = y_ref[:, cs] * _silu(z_ref[:, cs])
        v = v * lax.rsqrt(jnp.mean(v * v, -1, keepdims=True) + RMS_EPS) * nw_ref[:, cs]
        part = jnp.dot(v.astype(BF16), w_ref[cs, :], preferred_element_type=F32)
        acc = part if acc is None else acc + part
    out = _ln_rows(ALPHA * x_ref[...] + acc, g_ref[...], b_ref[...])
    o_ref[...] = out
    _store_row_tiles(ot_ref, out)


def _router_kernel(x_ref, w_ref, b_ref, idx_ref, gate_ref, rank_ref, cnt_ref, carry, *, tm):
    i = pl.program_id(0)

    @pl.when(i == 0)
    def _():
        carry[...] = jnp.zeros(carry.shape, F32)

    logits = jnp.dot(x_ref[...].astype(BF16), w_ref[...].astype(BF16), preferred_element_type=F32) + b_ref[...]
    lane = lax.broadcasted_iota(I32, (tm, N_EXPERTS), 1)
    work = logits
    vals, hots = [], []
    for _ in range(TOP_K):
        m = jnp.max(work, axis=1, keepdims=True)
        ik = jnp.min(jnp.where(work == m, lane, N_EXPERTS), axis=1, keepdims=True)
        hot = lane == ik
        vals.append(m)
        hots.append(hot)
        work = jnp.where(hot, NEG_INF, work)
    es = [jnp.exp(v - vals[0]) for v in vals]
    den = es[0] + es[1] + es[2] + es[3]
    hot_all = (hots[0] | hots[1] | hots[2] | hots[3])
    r_i = lax.broadcasted_iota(I32, (tm, tm), 0)
    c_i = lax.broadcasted_iota(I32, (tm, tm), 1)
    before = (r_i > c_i).astype(BF16)
    pre = jnp.dot(before, hot_all.astype(BF16), preferred_element_type=F32) + carry[...]
    lane4 = lax.broadcasted_iota(I32, (tm, TOP_K), 1)
    idx = jnp.zeros((tm, TOP_K), I32)
    gate = jnp.zeros((tm, TOP_K), F32)
    rank = jnp.zeros((tm, TOP_K), I32)
    for k in range(TOP_K):
        ik = jnp.sum(jnp.where(hots[k], lane, 0), axis=1, keepdims=True)
        rk = jnp.sum(jnp.where(hots[k], pre, 0.0), axis=1, keepdims=True).astype(I32)
        idx = jnp.where(lane4 == k, ik, idx)
        rank = jnp.where(lane4 == k, rk, rank)
        gate = jnp.where(lane4 == k, es[k] / den, gate)
    idx_ref[...] = idx
    gate_ref[...] = gate
    rank_ref[...] = rank
    carry[...] = carry[...] + jnp.sum(hot_all.astype(F32), axis=0, keepdims=True)
    cnt_ref[...] = carry[...]


def _router(x_all, w, b):
    n = x_all.shape[0]
    tm = 384 if n % 384 == 0 else 128
    out4 = lambda dt: jax.ShapeDtypeStruct((n, TOP_K), dt)
    return pl.pallas_call(
        functools.partial(_router_kernel, tm=tm),
        grid=(n // tm,),
        in_specs=[pl.BlockSpec((tm, D_MODEL), lambda i: (i, 0)),
                  pl.BlockSpec((D_MODEL, N_EXPERTS), lambda i: (0, 0)),
                  pl.BlockSpec((1, N_EXPERTS), lambda i: (0, 0))],
        out_specs=[pl.BlockSpec((tm, TOP_K), lambda i: (i, 0))] * 3 + [pl.BlockSpec((1, N_EXPERTS), lambda i: (0, 0))],
        out_shape=[out4(I32), out4(F32), out4(I32), jax.ShapeDtypeStruct((1, N_EXPERTS), F32)],
        scratch_shapes=[pltpu.VMEM((1, N_EXPERTS), F32)],
        compiler_params=_cp("arbitrary"),
        name="router",
    )(x_all, w, b)


def _tile_copy(src_hbm, row8, dst, dst_row8, sem):
    return pltpu.make_async_copy(src_hbm.at[pl.ds(pl.multiple_of(row8, ROW_TILES), ROW_TILES), :],
                                 dst.at[pl.ds(pl.multiple_of(dst_row8, ROW_TILES), ROW_TILES), :], sem)


def _ffn_kernel(te_ref, tos_ref, xt_hbm, w1_ref, b1_ref, w2_ref, b2_ref, ot_ref,
                xbuf, sem, w1b, w2i, w2b, *, rows):
    i = pl.program_id(0)
    last = pl.num_programs(0) - 1
    slot = i % 2
    half = D_FF // 2

    def start_rows(tile, s, r0, r1):
        for r in range(r0, r1):
            _tile_copy(xt_hbm, tos_ref[tile * rows + r], xbuf.at[s], r * ROW_TILES, sem.at[s]).start(priority=r % 2)

    def wait_rows(s):
        pltpu.make_async_copy(xt_hbm.at[pl.ds(0, rows * ROW_TILES), :], xbuf.at[s], sem.at[s]).wait()

    @pl.when(i == 0)
    def _():
        start_rows(0, 0, 0, rows)

    @pl.when((i == 0) | (te_ref[i] != te_ref[jnp.maximum(i - 1, 0)]))
    def _():
        w1b[...] = w1_ref[...].astype(BF16)
        for c in range(ROW_TILES):
            cs = slice(c * LANES, (c + 1) * LANES)
            w2i[c, pl.ds(0, half, stride=2), :] = w2_ref[0:half, cs]
            w2i[c, pl.ds(1, half, stride=2), :] = w2_ref[half:D_FF, cs]
            w2b[:, cs] = w2i[c].astype(BF16)

    wait_rows(slot)
    n_chunks = D_FF // LANES
    per = rows // (n_chunks + 2)
    start_rows(i + 1, 1 - slot, 0, per)
    xb = _load_row_tiles(xbuf.at[slot], rows).astype(BF16)
    h = jnp.dot(xb, w1b[...], preferred_element_type=F32) + b1_ref[...]
    even = (lax.broadcasted_iota(I32, (rows, LANES), 1) % 2) == 0
    parts = []
    for c in range(n_chunks):
        start_rows(i + 1, 1 - slot, (c + 1) * per, (c + 2) * per)
        ha = h[:, c * LANES:(c + 1) * LANES]
        hb = h[:, D_FF + c * LANES:D_FF + (c + 1) * LANES]
        glu = jnp.minimum(jnp.where(even, ha, pltpu.roll(hb, 1, 1)), SWIGLU_LIMIT)
        lin = jnp.clip(jnp.where(even, pltpu.roll(ha, LANES - 1, 1), hb), -SWIGLU_LIMIT, SWIGLU_LIMIT)
        parts.append((glu * _sigmoid(SWIGLU_ALPHA * glu) * (lin + 1.0)).astype(BF16))
    act = jnp.concatenate(parts, axis=1)
    start_rows(i + 1, 1 - slot, (n_chunks + 1) * per, rows)
    _store_row_tiles(ot_ref, jnp.dot(act, w2b[...], preferred_element_type=F32) + b2_ref[...])

    @pl.when(i == last)
    def _():
        wait_rows(1 - slot)


def _ffn(xt_all, tile_expert, tok_of_slot, layer, w1, b1, w2, b2, ntiles):
    rows = MOE_ROWS
    wspec = lambda shape: pl.BlockSpec((None, None) + shape, lambda i, te, tos: (layer, te[i], 0, 0))
    grid_spec = pltpu.PrefetchScalarGridSpec(
        num_scalar_prefetch=2,
        grid=(ntiles,),
        in_specs=[pl.BlockSpec(memory_space=pl.ANY),
                  wspec((D_MODEL, 2 * D_FF)), wspec((1, 2 * D_FF)),
                  wspec((D_FF, D_MODEL)), wspec((1, D_MODEL))],
        out_specs=pl.BlockSpec((rows * ROW_TILES, LANES), lambda i, te, tos: (i, 0)),
        scratch_shapes=[pltpu.VMEM((2, rows * ROW_TILES, LANES), F32), pltpu.SemaphoreType.DMA((2,)),
                        pltpu.VMEM((D_MODEL, 2 * D_FF), BF16), pltpu.VMEM((ROW_TILES, D_FF, LANES), F32),
                        pltpu.VMEM((D_FF, D_MODEL), BF16)],
    )
    return pl.pallas_call(
        functools.partial(_ffn_kernel, rows=rows),
        grid_spec=grid_spec,
        out_shape=jax.ShapeDtypeStruct((ntiles * rows * ROW_TILES, LANES), F32),
        compiler_params=_cp("arbitrary"),
        name="moe_ffn",
    )(tile_expert, tok_of_slot, xt_all, w1, b1, w2, b2)


def _combine_kernel(pos_ref, ot_hbm, gate_ref, x_ref, g_ref, b_ref, out_ref, buf, sem, *, tm):
    i = pl.program_id(0)
    n = pl.num_programs(0)
    slot = i % 2

    def gather(tile, s):
        def issue(t, _):
            for k in range(TOP_K):
                _tile_copy(ot_hbm, pos_ref[(tile * tm + t) * TOP_K + k], buf.at[s, k], t * ROW_TILES,
                           sem.at[s]).start(priority=k % 2)
            return 0
        lax.fori_loop(0, tm, issue, 0, unroll=4)

    @pl.when(i == 0)
    def _():
        gather(0, 0)

    @pl.when(i + 1 < n)
    def _():
        gather(i + 1, 1 - slot)

    for k in range(TOP_K):
        pltpu.make_async_copy(ot_hbm.at[pl.ds(0, tm * ROW_TILES), :], buf.at[slot, k], sem.at[slot]).wait()
    f = None
    for k in range(TOP_K):
        part = gate_ref[:, k:k + 1] * _load_row_tiles(buf.at[slot, k], tm)
        f = part if f is None else f + part
    out_ref[...] = _ln_rows(ALPHA * x_ref[...] + f, g_ref[...], b_ref[...])


def _combine(pos_flat, ot_sorted, gate, x_all, g, b):
    n = x_all.shape[0]
    tm = 128
    grid_spec = pltpu.PrefetchScalarGridSpec(
        num_scalar_prefetch=1,
        grid=(n // tm,),
        in_specs=[pl.BlockSpec(memory_space=pl.ANY),
                  pl.BlockSpec((tm, TOP_K), lambda i, pos: (i, 0)),
                  pl.BlockSpec((tm, D_MODEL), lambda i, pos: (i, 0)),
                  pl.BlockSpec((1, D_MODEL), lambda i, pos: (0, 0)),
                  pl.BlockSpec((1, D_MODEL), lambda i, pos: (0, 0))],
        out_specs=pl.BlockSpec((tm, D_MODEL), lambda i, pos: (i, 0)),
        scratch_shapes=[pltpu.VMEM((2, TOP_K, tm * ROW_TILES, LANES), F32), pltpu.SemaphoreType.DMA((2,))],
    )
    return pl.pallas_call(
        functools.partial(_combine_kernel, tm=tm),
        grid_spec=grid_spec,
        out_shape=jax.ShapeDtypeStruct((n, D_MODEL), F32),
        compiler_params=_cp("arbitrary"),
        name="moe_combine",
    )(pos_flat, ot_sorted, gate, x_all, g, b)


def _moe(x_all, xt_all, layer, w_r, b_r, w1, b1, w2, b2, g, b):
    n = x_all.shape[0]
    rows = MOE_ROWS
    ntiles = -(-(n * TOP_K + N_EXPERTS * (rows - 1)) // rows)
    idx, gate, rank, cnt = _router(x_all, w_r, b_r.reshape(1, N_EXPERTS))
    counts = cnt[0].astype(I32)
    padded = ((counts + rows - 1) // rows) * rows
    ends = jnp.cumsum(padded)
    starts = ends - padded
    experts = jnp.arange(N_EXPERTS, dtype=I32)
    pos = rank + jnp.sum(jnp.where(idx[:, :, None] == experts, starts, 0), axis=-1)
    tile_starts = jnp.arange(ntiles, dtype=I32) * rows
    tile_expert = jnp.minimum(jnp.sum((tile_starts[:, None] >= ends[None, :]).astype(I32), axis=1), N_EXPERTS - 1)
    tok = jnp.broadcast_to(jnp.arange(n, dtype=I32)[:, None] * ROW_TILES, (n, TOP_K))
    tok_of_slot = jnp.zeros(((ntiles + 1) * rows,), I32).at[pos.reshape(-1)].set(tok.reshape(-1))
    ot_sorted = _ffn(xt_all, tile_expert, tok_of_slot, layer, w1, b1[:, :, None, :], w2, b2[:, :, None, :],
                     ntiles)
    return _combine(pos.reshape(-1) * ROW_TILES, ot_sorted, gate, x_all, g.reshape(1, -1), b.reshape(1, -1))


def _block_diag(w):
    nb, bw, _ = w.shape
    eye = jnp.eye(nb, dtype=w.dtype)
    return (eye[:, None, :, None] * w[:, :, None, :]).reshape(nb * bw, nb * bw)


def _mix_even(x_arr, out_row0, n_total, bsz, t, rg_conv, rg_h, cache, p, prev):
    n = bsz * t
    xa, ga, q, kv = _inproj_even(x_arr, 0, n, p["w_in"])
    ya, conv_new, h_last = _rglru(
        xa.reshape(bsz, t, RG_WIDTH), ga.reshape(bsz, t, RG_WIDTH), rg_conv, rg_h.reshape(bsz, 1, RG_WIDTH),
        p["rg_conv_w"], p["rg_conv_b"], p["rg_wg"], p["rg_bg"], p["rg_c"])
    q3 = q.reshape(bsz, t, B_WIDTH)
    kv3 = kv.reshape(bsz, t, 2 * B_WIDTH)
    if cache is None:
        yb = _attn_prompt(q3, kv3, p["rel_bias"])
    else:
        yb = _attn_sample(q3, cache, kv3, p["rel_bias"])
    x1 = _token_call(_outproj_even_kernel, "outproj_even", n, 0, n_total, out_row0,
                     [ya.reshape(n, RG_WIDTH), yb.reshape(n, B_WIDTH)], x_arr,
                     [p["w_out"], p["ln_g0"], p["ln_b0"]], prev)
    return x1, conv_new, h_last.reshape(bsz, RG_WIDTH), kv3


def _mix_odd(x_all, row0, bsz, t, ssd_conv, ssd_h, p, prev):
    n = bsz * t
    z, xbc, dt = _inproj_odd(x_all, row0, n, p["ssd_w_main"], p["ssd_w_dt"])
    h0t = jnp.transpose(ssd_h.astype(F32), (0, 3, 1, 2)).reshape(bsz, SSD_N, SSD_D_INNER)
    y, conv_new, h_t = _ssd(xbc.reshape(bsz, t, SSD_CONV_DIM), dt.reshape(bsz, t, LANES), ssd_conv, h0t,
                            p["ssd_conv_w"], p["ssd_conv_b"], p["ssd_dtb"], p["ssd_a"], p["ssd_dsk"])
    h_last = jnp.transpose(h_t.reshape(bsz, SSD_N, SSD_HEADS, SSD_P), (0, 2, 3, 1))
    x1 = _token_call(_outproj_odd_kernel, "outproj_odd", n, row0, x_all.shape[0], row0,
                     [y.reshape(n, SSD_D_INNER), z], x_all,
                     [p["ssd_norm_w"], p["ssd_w_out"], p["ln_g2"], p["ln_b2"]], prev)
    return x1, conv_new, h_last


def _mixer_params(rel_bias, w_in_mix, rg_conv_w, rg_conv_b, rg_w_a, rg_b_a, rg_w_i, rg_b_i, rg_lambda, w_out_mix,
                  ssd_w_in, ssd_conv_w, ssd_conv_b, ssd_dt_bias, ssd_a_log, ssd_d, ssd_norm_w, ssd_w_out, ln_g, ln_b):
    row2 = lambda v: v.reshape(1, -1).astype(F32)
    pad_heads = lambda v: jnp.pad(v.astype(F32), (0, LANES - SSD_HEADS)).reshape(1, LANES)
    return dict(
        rel_bias=rel_bias,
        w_in=w_in_mix[0].astype(BF16),
        rg_conv_w=rg_conv_w[0].astype(F32), rg_conv_b=row2(rg_conv_b[0]),
        rg_wg=jnp.concatenate([_block_diag(rg_w_a[0]), _block_diag(rg_w_i[0])], axis=1).astype(BF16),
        rg_bg=jnp.concatenate([rg_b_a[0].reshape(1, -1), rg_b_i[0].reshape(1, -1)], axis=1).astype(F32),
        rg_c=row2(-RG_C * jax.nn.softplus(-rg_lambda[0].astype(F32))),
        w_out=w_out_mix[0].astype(BF16),
        ln_g0=row2(ln_g[0, 0]), ln_b0=row2(ln_b[0, 0]),
        ssd_w_main=ssd_w_in[0][:, :SSD_MAIN].astype(BF16),
        ssd_w_dt=jnp.pad(ssd_w_in[0][:, SSD_MAIN:], ((0, 0), (0, LANES - SSD_HEADS))).astype(BF16),
        ssd_conv_w=ssd_conv_w[0].astype(F32), ssd_conv_b=row2(ssd_conv_b[0]),
        ssd_dtb=pad_heads(ssd_dt_bias[0]), ssd_a=pad_heads(-jnp.exp(ssd_a_log[0].astype(F32))),
        ssd_dsk=row2(jnp.repeat(ssd_d[0].astype(F32), SSD_P)),
        ssd_norm_w=row2(ssd_norm_w[0]), ssd_w_out=ssd_w_out[0].astype(BF16),
        ln_g2=row2(ln_g[1, 0]), ln_b2=row2(ln_b[1, 0]),
    )


def kernel(x_prompt, x_sample, state_rglru_conv, state_rglru_h, cache_swa_kv, state_ssd_conv, state_ssd_h,
           rel_bias, w_in_mix, rg_conv_w, rg_conv_b, rg_w_a, rg_b_a, rg_w_i, rg_b_i, rg_lambda, w_out_mix,
           ssd_w_in, ssd_conv_w, ssd_conv_b, ssd_dt_bias, ssd_a_log, ssd_d, ssd_norm_w, ssd_w_out,
           ln_g, ln_b, router_w, router_b, exp_w1, exp_b1, exp_w2, exp_b2):
    bp, tp, _ = x_prompt.shape
    bs, ts, _ = x_sample.shape
    n_p, n_s = bp * tp, bs * ts
    n_total = n_p + n_s
    p = _mixer_params(rel_bias, w_in_mix, rg_conv_w, rg_conv_b, rg_w_a, rg_b_a, rg_w_i, rg_b_i, rg_lambda, w_out_mix,
                      ssd_w_in, ssd_conv_w, ssd_conv_b, ssd_dt_bias, ssd_a_log, ssd_d, ssd_norm_w, ssd_w_out, ln_g, ln_b)

    xp = x_prompt.reshape(n_p, D_MODEL)
    xs = x_sample.reshape(n_s, D_MODEL)
    zeros = lambda *s: jnp.zeros(s, F32)
    x1, p_rg_conv, p_rg_h, p_kv = _mix_even(xp, 0, n_total, bp, tp, zeros(bp, CONV_W - 1, RG_WIDTH),
                                            zeros(bp, RG_WIDTH), None, p, None)
    cache = cache_swa_kv[0].reshape(bs, cache_swa_kv.shape[2], 2 * B_WIDTH)
    x1, s_rg_conv, s_rg_h, s_kv = _mix_even(xs, n_p, n_total, bs, ts, state_rglru_conv[0], state_rglru_h[0],
                                            cache, p, x1)
    x2 = _moe(x1[0], x1[1], 0, router_w[0].astype(F32), router_b[0].astype(F32), exp_w1, exp_b1, exp_w2, exp_b2,
              ln_g[0, 1], ln_b[0, 1])

    x3, p_ssd_conv, p_ssd_h = _mix_odd(x2, 0, bp, tp, zeros(bp, CONV_W - 1, SSD_CONV_DIM),
                                       zeros(bp, SSD_HEADS, SSD_P, SSD_N), p, None)
    x3, s_ssd_conv, s_ssd_h = _mix_odd(x2, n_p, bs, ts, state_ssd_conv[0], state_ssd_h[0], p, x3)
    x4 = _moe(x3[0], x3[1], 1, router_w[1].astype(F32), router_b[1].astype(F32), exp_w1, exp_b1, exp_w2, exp_b2,
              ln_g[1, 1], ln_b[1, 1])

    kv_shape = lambda b, t: (1, b, t, 2, B_HEADS, B_HD)
    return (x4[:n_p].reshape(bp, tp, D_MODEL), x4[n_p:].reshape(bs, ts, D_MODEL),
            p_rg_conv[None], p_rg_h[None], p_kv[:, -min(MAX_WINDOW, tp):].reshape(kv_shape(bp, min(MAX_WINDOW, tp))),
            p_ssd_conv[None], p_ssd_h[None],
            s_rg_conv[None], s_rg_h[None], s_kv.reshape(kv_shape(bs, ts)),
            s_ssd_conv[None], s_ssd_h[None])
```

```python
import functools
import math

import numpy as np
import jax
import jax.numpy as jnp
from jax import lax
from jax.experimental import pallas as pl
from jax.experimental.pallas import tpu as pltpu

F32 = jnp.float32
BF16 = jnp.bfloat16
I32 = jnp.int32

D_MODEL = 1024
DEPTH = 2
ALPHA = (2 * DEPTH) ** 0.25
LN_EPS = 1e-5
RMS_EPS = 1e-5
CONV_W = 4
RG_BLOCKS = 12
RG_BW = 64
RG_WIDTH = RG_BLOCKS * RG_BW
RG_C = 8.0
B_HEADS = 12
B_HD = 64
B_WIDTH = B_HEADS * B_HD
PATTERNS = ((128, 1), (512, 4), (2048, 16))
MAX_WINDOW = 2048
N_BUCKETS = 32
MAX_DISTANCE = MAX_WINDOW
EVEN_IN = 2 * RG_WIDTH + 3 * B_WIDTH
SSD_D_INNER = 2 * D_MODEL
SSD_P = 64
SSD_HEADS = SSD_D_INNER // SSD_P
SSD_N = 128
SSD_GROUPS = 4
SSD_GN = SSD_GROUPS * SSD_N
SSD_CONV_DIM = SSD_D_INNER + 2 * SSD_GN
SSD_MAIN = SSD_D_INNER + SSD_CONV_DIM
N_EXPERTS = 32
TOP_K = 4
D_FF = D_MODEL
SWIGLU_LIMIT = 7.0
SWIGLU_ALPHA = 1.702

LANES = 128
VMEM_LIMIT = 56 * 1024 * 1024

SSD_Q = 128
ATT_BLK = 256
ATT_CHUNK = 64
MOE_ROWS = 256
NEG_INF = float("-inf")


ROW_TILES = D_MODEL // LANES


def _cp(*sem):
    return pltpu.CompilerParams(dimension_semantics=sem, vmem_limit_bytes=VMEM_LIMIT)


def _store_row_tiles(ref, val):
    rows = val.shape[0]
    for s in range(ROW_TILES):
        ref[pl.ds(s, rows, stride=ROW_TILES), :] = val[:, s * LANES:(s + 1) * LANES]


def _load_row_tiles(ref, rows):
    return jnp.concatenate([ref[pl.ds(s, rows, stride=ROW_TILES), :] for s in range(ROW_TILES)], axis=1)


def _ln_rows(v, g, b):
    mu = jnp.mean(v, -1, keepdims=True)
    xc = v - mu
    var = jnp.mean(xc * xc, -1, keepdims=True)
    return xc * lax.rsqrt(var + LN_EPS) * g + b


def _bf16_round(x):
    return x.astype(BF16).astype(F32)


def _sigmoid(x):
    return 1.0 / (1.0 + jnp.exp(-x))


def _silu(x):
    return x * _sigmoid(x)


def _softplus(x):
    return jnp.maximum(x, 0.0) + jnp.log1p(jnp.exp(-jnp.abs(x)))


def _gelu_tanh(x):
    return 0.5 * x * (1.0 + jnp.tanh(math.sqrt(2.0 / math.pi) * (x + 0.044715 * (x * x * x))))


def _row_tile(n):
    return min(n, 256)


def _inproj_even_kernel(x_ref, w_ref, xa_ref, ga_ref, q_ref, kv_ref):
    xb = x_ref[...].astype(BF16)
    c0, c1, c2 = RG_WIDTH, 2 * RG_WIDTH, 2 * RG_WIDTH + B_WIDTH
    xa_ref[...] = jnp.dot(xb, w_ref[:, 0:c0], preferred_element_type=F32)
    ga_ref[...] = jnp.dot(xb, w_ref[:, c0:c1], preferred_element_type=F32)
    q_ref[...] = jnp.dot(xb, w_ref[:, c1:c2], preferred_element_type=F32)
    kv_ref[...] = jnp.dot(xb, w_ref[:, c2:EVEN_IN], preferred_element_type=F32)


def _inproj_even(x_all, row0, n, w):
    tm = _row_tile(n)
    off = row0 // tm
    widths = (RG_WIDTH, RG_WIDTH, B_WIDTH, 2 * B_WIDTH)
    return pl.pallas_call(
        _inproj_even_kernel,
        grid=(n // tm,),
        in_specs=[pl.BlockSpec((tm, D_MODEL), lambda i: (i + off, 0)),
                  pl.BlockSpec((D_MODEL, EVEN_IN), lambda i: (0, 0))],
        out_specs=[pl.BlockSpec((tm, c), lambda i: (i, 0)) for c in widths],
        out_shape=[jax.ShapeDtypeStruct((n, c), F32) for c in widths],
        compiler_params=_cp("parallel"),
        name="inproj_even",
    )(x_all, w)


def _rglru_kernel(xa_ref, ga_ref, cbuf_ref, h0_ref, cw_ref, cb_ref, wg_ref, bg_ref, c_ref,
                  ya_ref, cnew_ref, hlast_ref, xx, a_s, u_s, hc, *, tt):
    j = pl.program_id(1)
    hist = CONV_W - 1
    base = 8

    @pl.when(j == 0)
    def _():
        xx[base - hist:base, :] = cbuf_ref[...]
        hc[...] = h0_ref[...]

    @pl.when(j > 0)
    def _():
        xx[base - hist:base, :] = xx[base + tt - hist:base + tt, :]

    xx[base:base + tt, :] = xa_ref[...]
    conv = cb_ref[...]
    for k in range(CONV_W):
        conv = conv + cw_ref[k:k + 1, :] * _bf16_round(xx[base - hist + k:base - hist + k + tt, :])
    gates = jnp.dot(conv.astype(BF16), wg_ref[...], preferred_element_type=F32) + bg_ref[...]
    r = _sigmoid(gates[:, :RG_WIDTH])
    ig = _sigmoid(gates[:, RG_WIDTH:])
    log_a = c_ref[...] * r
    a = jnp.exp(log_a)
    a_s[...] = a
    u_s[...] = jnp.sqrt(-jnp.tanh(log_a) * (a * a + 1.0)) * (ig * conv)

    def step(t, h):
        h = a_s[pl.ds(t, 1), :] * h + u_s[pl.ds(t, 1), :]
        u_s[pl.ds(t, 1), :] = h
        return h

    h = lax.fori_loop(0, tt, step, hc[...], unroll=min(tt, 8))
    hc[...] = h
    ya_ref[...] = _gelu_tanh(ga_ref[...]) * u_s[...]

    @pl.when(j == pl.num_programs(1) - 1)
    def _():
        hlast_ref[...] = h
        cnew_ref[...] = xx[base + tt - hist:base + tt, :]


def _rglru(xa, ga, cbuf, h0, cw, cb, wg, bg, cvec):
    bsz, t, c = xa.shape
    tt = min(t, 256)
    row = lambda b, j: (b, j, 0)
    per_b = lambda b, j: (b, 0, 0)
    fixed = lambda b, j: (0, 0)
    return pl.pallas_call(
        functools.partial(_rglru_kernel, tt=tt),
        grid=(bsz, t // tt),
        in_specs=[pl.BlockSpec((None, tt, c), row), pl.BlockSpec((None, tt, c), row),
                  pl.BlockSpec((None, CONV_W - 1, c), per_b), pl.BlockSpec((None, 1, c), per_b),
                  pl.BlockSpec((CONV_W, c), fixed), pl.BlockSpec((1, c), fixed),
                  pl.BlockSpec((c, 2 * c), fixed), pl.BlockSpec((1, 2 * c), fixed),
                  pl.BlockSpec((1, c), fixed)],
        out_specs=[pl.BlockSpec((None, tt, c), row), pl.BlockSpec((None, CONV_W - 1, c), per_b),
                   pl.BlockSpec((None, 1, c), per_b)],
        out_shape=[jax.ShapeDtypeStruct((bsz, t, c), F32),
                   jax.ShapeDtypeStruct((bsz, CONV_W - 1, c), F32),
                   jax.ShapeDtypeStruct((bsz, 1, c), F32)],
        scratch_shapes=[pltpu.VMEM((8 + tt, c), F32), pltpu.VMEM((tt, c), F32),
                        pltpu.VMEM((tt, c), F32), pltpu.VMEM((1, c), F32)],
        compiler_params=_cp("parallel", "arbitrary"),
        name="rglru",
    )(xa, ga, cbuf, h0, cw, cb, wg, bg, cvec)


def _t5_bucket(dist):
    max_exact = N_BUCKETS // 2
    safe = np.maximum(dist, 1)
    large = max_exact + (np.log(safe / max_exact) / np.log(MAX_DISTANCE / max_exact)
                         * (N_BUCKETS - max_exact)).astype(np.int32)
    return np.where(dist < max_exact, dist, np.minimum(large, N_BUCKETS - 1)).astype(np.int32)


def _distance_bias(rel_bias, max_d):
    d = np.arange(max_d + 1)
    mult = np.zeros(max_d + 1, np.float64)
    for w, dil in PATTERNS:
        mult += ((d % dil == 0) & (d <= w)).astype(np.float64)
    logm = np.where(mult > 0, np.log(np.maximum(mult, 1.0)), -np.inf).astype(np.float32)
    vals = rel_bias.astype(F32)[_t5_bucket(d)].T + jnp.asarray(logm)[None, :]
    return jnp.concatenate([vals, jnp.full((vals.shape[0], 1), NEG_INF, F32)], axis=1)


def _attn_prompt_kernel(q_ref, k_ref, v_ref, rv_ref, o_ref, tab_ref, k_s, v_s, *, blk, nq):
    qi = pl.program_id(2)

    @pl.when((pl.program_id(1) == 0) & (qi == 0))
    def _():
        for h in range(2):
            for d in range(nq):
                win = rv_ref[h:h + 1, (nq - 1 - d) * blk:(nq + 1 - d) * blk]
                rolled = pltpu.roll(jnp.broadcast_to(win, (blk, 2 * blk)), 0, 1, stride=1, stride_axis=0)
                tab_ref[h, d] = rolled[:, blk:]

    lo = lax.broadcasted_iota(I32, (blk, LANES), 1) < B_HD

    @pl.when(qi == 0)
    def _():
        k_s[...] = k_ref[...].astype(BF16)
        v_s[...] = v_ref[...].astype(BF16)

    q2 = q_ref[...] * (B_HD ** -0.5)
    qh = (jnp.where(lo, q2, 0.0).astype(BF16), jnp.where(lo, 0.0, q2).astype(BF16))
    nt = (((1,), (1,)), ((), ()))
    ch = ATT_CHUNK

    def scores(d):
        k = k_s[pl.ds(pl.multiple_of((qi - d) * blk, blk), blk), :]
        return tuple(lax.dot_general(qh[h], k, nt, preferred_element_type=F32) for h in range(2))

    def body(d, carry):
        m0, l0, m1, l1, acc, s0, s1 = carry
        s_next = scores(jnp.minimum(d + 1, qi))
        v = v_s[pl.ds(pl.multiple_of((qi - d) * blk, blk), blk), :]
        new = []
        for h, (m, l, s) in enumerate(((m0, l0, s0), (m1, l1, s1))):
            mns, lns, als, ps = [], [], [], []
            for c in range(blk // ch):
                rs = slice(c * ch, (c + 1) * ch)
                sc = s[rs] + tab_ref[h, d, rs, :]
                mn = jnp.maximum(m[rs], jnp.broadcast_to(jnp.max(sc, axis=1, keepdims=True), (ch, LANES)))
                p = jnp.concatenate([jnp.exp(sc[:, j * LANES:(j + 1) * LANES] - mn) for j in range(blk // LANES)], axis=1)
                al = jnp.exp(m[rs] - mn)
                mns.append(mn)
                als.append(al)
                lns.append(al * l[rs] + jnp.broadcast_to(jnp.sum(p, axis=1, keepdims=True), (ch, LANES)))
                ps.append(p.astype(BF16))
            cat = lambda xs: jnp.concatenate(xs, axis=0)
            new.append((cat(mns), cat(lns), cat(als), jnp.dot(cat(ps), v, preferred_element_type=F32)))
        (m0, l0, a0, pv0), (m1, l1, a1, pv1) = new
        acc = jnp.where(lo, a0, a1) * acc + jnp.where(lo, pv0, pv1)
        return (m0, l0, m1, l1, acc) + s_next

    minf = jnp.full((blk, LANES), NEG_INF, F32)
    zero = jnp.zeros((blk, LANES), F32)
    out = lax.fori_loop(0, qi + 1, body, (minf, zero, minf, zero, zero) + scores(0))
    m0, l0, m1, l1, acc = out[:5]
    o_ref[...] = acc / jnp.where(lo, l0, l1)


def _attn_prompt(q, kv, rel_bias):
    bsz, t, _ = q.shape
    blk = min(ATT_BLK, t)
    nq = t // blk
    npair = B_HEADS // 2
    vals = _distance_bias(rel_bias, t)[:, :t + 1]
    rv = jnp.concatenate([jnp.full((B_HEADS, blk - 1), NEG_INF, F32), vals], axis=1)[:, ::-1]
    rv = rv.reshape(npair, 2, t + blk)
    return pl.pallas_call(
        functools.partial(_attn_prompt_kernel, blk=blk, nq=nq),
        grid=(npair, bsz, nq),
        in_specs=[pl.BlockSpec((None, blk, LANES), lambda p, b, i: (b, i, p)),
                  pl.BlockSpec((None, t, LANES), lambda p, b, i: (b, 0, p)),
                  pl.BlockSpec((None, t, LANES), lambda p, b, i: (b, 0, npair + p)),
                  pl.BlockSpec((None, 2, t + blk), lambda p, b, i: (p, 0, 0))],
        out_specs=pl.BlockSpec((None, blk, LANES), lambda p, b, i: (b, i, p)),
        out_shape=jax.ShapeDtypeStruct((bsz, t, B_WIDTH), F32),
        scratch_shapes=[pltpu.VMEM((2, nq, blk, blk), F32), pltpu.VMEM((t, LANES), BF16),
                        pltpu.VMEM((t, LANES), BF16)],
        compiler_params=_cp("arbitrary", "arbitrary", "arbitrary"),
        name="attn_prompt",
    )(q, kv, kv, rv)


ATT_S_ROWS = 16


def _attn_sample_kernel(qbd_ref, kc_ref, vc_ref, kn_ref, vn_ref, tabc_ref, tabn_ref, mask_ref, o_ref,
                        m_s, l_s, acc_s, *, t_new):
    c = pl.program_id(1)
    rows = t_new * ATT_S_ROWS
    nt = (((1,), (1,)), ((), ()))

    @pl.when(c == 0)
    def _():
        m_s[...] = jnp.full((rows, 1), NEG_INF, F32)
        l_s[...] = jnp.zeros((rows, 1), F32)
        acc_s[...] = jnp.zeros((rows, B_WIDTH), F32)

    qf = qbd_ref[...] * (B_HD ** -0.5)
    qb = qf.astype(BF16)
    s = lax.dot_general(qb, kc_ref[...].astype(BF16), nt, preferred_element_type=F32) + tabc_ref[...]
    m = m_s[...]
    mn = jnp.maximum(m, jnp.max(s, axis=1, keepdims=True))
    last = c == pl.num_programs(1) - 1

    sn = [jnp.sum(qf * kn_ref[j:j + 1, :], axis=1, keepdims=True) + tabn_ref[:, j:j + 1] for j in range(t_new)]
    sn = [jnp.where(last, x, NEG_INF) for x in sn]
    for x in sn:
        mn = jnp.maximum(mn, x)
    safe = jnp.where(mn == NEG_INF, 0.0, mn)
    p = jnp.exp(s - safe)
    al = jnp.exp(m - safe)
    l = al * l_s[...] + jnp.sum(p, axis=1, keepdims=True)
    acc = al * acc_s[...] + jnp.dot(p.astype(BF16), vc_ref[...].astype(BF16), preferred_element_type=F32)
    for j, x in enumerate(sn):
        pj = jnp.exp(x - safe)
        l = l + pj
        acc = acc + pj * vn_ref[j:j + 1, :]
    m_s[...] = mn
    l_s[...] = l
    acc_s[...] = acc

    @pl.when(last)
    def _():
        om = (acc / l) * mask_ref[...]
        for t in range(t_new):
            o_ref[t:t + 1, :] = jnp.sum(om[t * ATT_S_ROWS:(t + 1) * ATT_S_ROWS, :], axis=0, keepdims=True)


def _attn_sample(q, cache, kv_new, rel_bias):
    bsz, t_new, _ = q.shape
    w = cache.shape[1]
    rows = t_new * ATT_S_ROWS
    vals = _distance_bias(rel_bias, MAX_WINDOW)
    vals = jnp.concatenate([vals, jnp.zeros((ATT_S_ROWS - B_HEADS, vals.shape[1]), F32)], axis=0)
    tq = np.repeat(np.arange(t_new), ATT_S_ROWS)
    hq = np.tile(np.arange(ATT_S_ROWS), t_new)
    ext = jnp.concatenate([vals[:, :MAX_WINDOW + 1], jnp.full((ATT_S_ROWS, w + t_new - MAX_WINDOW), NEG_INF, F32)], axis=1)
    ext = ext.at[B_HEADS:, :].set(0.0)
    tabc = jnp.stack([ext[:, t + 1:w + t + 1][:, ::-1] for t in range(t_new)]).reshape(rows, w)
    dist_n = tq[:, None] - np.arange(t_new)[None, :]
    dist_n = np.where(dist_n < 0, MAX_WINDOW + 1, dist_n)
    tabn = vals[jnp.asarray(hq)[:, None], jnp.asarray(dist_n)]
    mask = (hq[:, None] == (np.arange(B_WIDTH)[None, :] // B_HD)).astype(np.float32)
    qbd = (q[:, :, None, :] * jnp.asarray(mask.reshape(t_new, ATT_S_ROWS, B_WIDTH))[None]).reshape(bsz, rows, B_WIDTH)
    kc = 1024
    return pl.pallas_call(
        functools.partial(_attn_sample_kernel, t_new=t_new),
        grid=(bsz, w // kc),
        in_specs=[pl.BlockSpec((None, rows, B_WIDTH), lambda b, c: (b, 0, 0)),
                  pl.BlockSpec((None, kc, B_WIDTH), lambda b, c: (b, c, 0)),
                  pl.BlockSpec((None, kc, B_WIDTH), lambda b, c: (b, c, 1)),
                  pl.BlockSpec((None, t_new, B_WIDTH), lambda b, c: (b, 0, 0)),
                  pl.BlockSpec((None, t_new, B_WIDTH), lambda b, c: (b, 0, 1)),
                  pl.BlockSpec((rows, kc), lambda b, c: (0, c)),
                  pl.BlockSpec((rows, t_new), lambda b, c: (0, 0)),
                  pl.BlockSpec((rows, B_WIDTH), lambda b, c: (0, 0))],
        out_specs=pl.BlockSpec((None, t_new, B_WIDTH), lambda b, c: (b, 0, 0)),
        out_shape=jax.ShapeDtypeStruct((bsz, t_new, B_WIDTH), F32),
        scratch_shapes=[pltpu.VMEM((rows, 1), F32), pltpu.VMEM((rows, 1), F32), pltpu.VMEM((rows, B_WIDTH), F32)],
        compiler_params=_cp("parallel", "arbitrary"),
        name="attn_sample",
    )(qbd, cache, cache, kv_new, kv_new, tabc, tabn, jnp.asarray(mask))


def _outproj_even_kernel(*refs):
    ya_ref, yb_ref, x_ref, w_ref, g_ref, b_ref = refs[:6]
    o_ref, ot_ref = refs[-2:]
    y = jnp.dot(ya_ref[...].astype(BF16), w_ref[0:RG_WIDTH, :], preferred_element_type=F32)
    y = y + jnp.dot(yb_ref[...].astype(BF16), w_ref[RG_WIDTH:, :], preferred_element_type=F32)
    out = _ln_rows(ALPHA * x_ref[...] + y, g_ref[...], b_ref[...])
    o_ref[...] = out
    _store_row_tiles(ot_ref, out)


def _token_call(kernel_fn, name, n, x_off_rows, out_rows, out_off_rows, row_inputs, x_arr, fixed_inputs, prev):
    tm = _row_tile(n)
    nt = n // tm
    xo, oo = x_off_rows // tm, out_off_rows // tm
    tail = prev is None and out_rows > out_off_rows + n
    if tail:
        assert out_off_rows == 0 and out_rows - n <= tm
        body = kernel_fn

        def kernel_fn(*refs):
            step = pl.program_id(0)
            pl.when(step < nt)(lambda: body(*refs))

            @pl.when(step >= nt)
            def _():
                refs[-2][...] = jnp.zeros(refs[-2].shape, F32)
                refs[-1][...] = jnp.zeros(refs[-1].shape, F32)

    src = lambda i: jnp.minimum(i, nt - 1)
    in_specs = [pl.BlockSpec((tm, a.shape[1]), lambda i: (src(i), 0)) for a in row_inputs]
    in_specs.append(pl.BlockSpec((tm, D_MODEL), lambda i: (src(i) + xo, 0)))
    in_specs += [pl.BlockSpec(a.shape, lambda i, nd=a.ndim: (0,) * nd) for a in fixed_inputs]
    args = list(row_inputs) + [x_arr] + list(fixed_inputs)
    aliases = {}
    if prev is not None:
        in_specs += [pl.BlockSpec(memory_space=pl.ANY)] * 2
        aliases = {len(args): 0, len(args) + 1: 1}
        args += list(prev)
    return pl.pallas_call(
        kernel_fn,
        grid=(nt + int(tail),),
        in_specs=in_specs,
        out_specs=[pl.BlockSpec((tm, D_MODEL), lambda i: (i + oo, 0)),
                   pl.BlockSpec((tm * ROW_TILES, LANES), lambda i: (i + oo, 0))],
        out_shape=[jax.ShapeDtypeStruct((out_rows, D_MODEL), F32),
                   jax.ShapeDtypeStruct((out_rows * ROW_TILES, LANES), F32)],
        input_output_aliases=aliases,
        compiler_params=_cp("parallel"),
        name=name,
    )(*args)


def _inproj_odd_kernel(x_ref, w_ref, wdt_ref, z_ref, xbc_ref, dt_ref):
    xb = x_ref[...].astype(BF16)
    z_ref[...] = jnp.dot(xb, w_ref[:, 0:SSD_D_INNER], preferred_element_type=F32)
    xbc_ref[...] = jnp.dot(xb, w_ref[:, SSD_D_INNER:SSD_MAIN], preferred_element_type=F32)
    dt_ref[...] = jnp.dot(xb, wdt_ref[...], preferred_element_type=F32)


def _inproj_odd(x_all, row0, n, w, wdt):
    tm = _row_tile(n)
    off = row0 // tm
    widths = (SSD_D_INNER, SSD_CONV_DIM, LANES)
    return pl.pallas_call(
        _inproj_odd_kernel,
        grid=(n // tm,),
        in_specs=[pl.BlockSpec((tm, D_MODEL), lambda i: (i + off, 0)),
                  pl.BlockSpec((D_MODEL, SSD_MAIN), lambda i: (0, 0)),
                  pl.BlockSpec((D_MODEL, LANES), lambda i: (0, 0))],
        out_specs=[pl.BlockSpec((tm, c), lambda i: (i, 0)) for c in widths],
        out_shape=[jax.ShapeDtypeStruct((n, c), F32) for c in widths],
        compiler_params=_cp("parallel"),
        name="inproj_odd",
    )(x_all, w, wdt)


def _ssd_kernel(xbc_ref, dt_ref, cbuf_ref, h0_ref, cw_ref, cb_ref, dtb_ref, a_ref, dsk_ref,
                y_ref, cnew_ref, hT_ref, xx, dq, S, *, tt):
    c = pl.program_id(1)
    q = SSD_Q
    hist = CONV_W - 1
    base = 8

    @pl.when(c == 0)
    def _():
        if tt < q:
            xx[...] = jnp.zeros(xx.shape, F32)
            dq[...] = jnp.zeros(dq.shape, F32)
        xx[base - hist:base, :] = cbuf_ref[...]
        S[...] = h0_ref[...]

    @pl.when(c > 0)
    def _():
        xx[base - hist:base, :] = xx[base + tt - hist:base + tt, :]

    xx[base:base + tt, :] = xbc_ref[...]
    dq[0:tt, :] = dt_ref[...]

    @pl.when(c == pl.num_programs(1) - 1)
    def _():
        cnew_ref[...] = xx[base + tt - hist:base + tt, :]

    def conv_cols(lo, hi):
        acc = cb_ref[:, lo:hi]
        for k in range(CONV_W):
            acc = acc + cw_ref[k:k + 1, lo:hi] * _bf16_round(xx[base - hist + k:base - hist + k + q, lo:hi])
        return _silu(acc)

    row = lax.broadcasted_iota(I32, (q, q), 0)
    col = lax.broadcasted_iota(I32, (q, q), 1)
    causal = row >= col
    dt = _softplus(dq[...] + dtb_ref[...])
    if tt < q:
        dt = jnp.where(lax.broadcasted_iota(I32, (q, LANES), 0) < tt, dt, 0.0)
    da = dt * a_ref[...]
    cum = jnp.dot(causal.astype(F32), da, precision=lax.Precision.HIGHEST, preferred_element_type=F32)
    cum_t = cum.T
    dt_t = dt.T
    cum_last = cum[q - 1:q, :]
    ecum = jnp.exp(cum)
    dtde = dt * jnp.exp(cum_last - cum)
    cd = jnp.exp(cum_last)
    lo = lax.broadcasted_iota(I32, (q, LANES), 1) < SSD_P
    lo1 = lo[0:1, :]
    nt = (((1,), (1,)), ((), ()))
    pairs_per_group = SSD_HEADS // SSD_GROUPS // 2

    for g in range(SSD_GROUPS):
        bm = conv_cols(SSD_D_INNER + g * SSD_N, SSD_D_INNER + (g + 1) * SSD_N)
        cm = conv_cols(SSD_D_INNER + SSD_GN + g * SSD_N, SSD_D_INNER + SSD_GN + (g + 1) * SSD_N)
        bmb, cmb = bm.astype(BF16), cm.astype(BF16)
        cb = lax.dot_general(cmb, bmb, nt, preferred_element_type=F32)
        bm_t = bm.T.astype(BF16)
        for pp in range(pairs_per_group):
            p = g * pairs_per_group + pp
            h0, h1 = 2 * p, 2 * p + 1
            cs = slice(p * LANES, (p + 1) * LANES)
            xs = conv_cols(p * LANES, (p + 1) * LANES)
            xsb = xs.astype(BF16)
            yd = []
            for h in (h0, h1):
                seg = cum[:, h:h + 1] - cum_t[h:h + 1, :]
                m = cb * jnp.where(causal, jnp.exp(jnp.where(causal, seg, 0.0)), 0.0) * dt_t[h:h + 1, :]
                yd.append(jnp.dot(m.astype(BF16), xsb, preferred_element_type=F32))
            s_old = S[:, cs]
            yo = jnp.dot(cmb, s_old.astype(BF16), preferred_element_type=F32)
            yo = yo * jnp.where(lo, ecum[:, h0:h0 + 1], ecum[:, h1:h1 + 1])
            y = jnp.where(lo, yd[0], yd[1]) + yo + dsk_ref[:, cs] * xs
            y_ref[:, cs] = y[0:tt, :]
            wts = jnp.where(lo, dtde[:, h0:h0 + 1], dtde[:, h1:h1 + 1])
            upd = jnp.dot(bm_t, (xs * wts).astype(BF16), preferred_element_type=F32)
            S[:, cs] = s_old * jnp.where(lo1, cd[:, h0:h0 + 1], cd[:, h1:h1 + 1]) + upd

    @pl.when(c == pl.num_programs(1) - 1)
    def _():
        hT_ref[...] = S[...]


def _ssd(xbc, dt, cbuf, h0t, cw, cb, dtb, a, dsk):
    bsz, t, _ = xbc.shape
    tt = min(t, SSD_Q)
    row = lambda b, c: (b, c, 0)
    per_b = lambda b, c: (b, 0, 0)
    fixed = lambda b, c: (0, 0)
    return pl.pallas_call(
        functools.partial(_ssd_kernel, tt=tt),
        grid=(bsz, t // tt),
        in_specs=[pl.BlockSpec((None, tt, SSD_CONV_DIM), row), pl.BlockSpec((None, tt, LANES), row),
                  pl.BlockSpec((None, CONV_W - 1, SSD_CONV_DIM), per_b),
                  pl.BlockSpec((None, SSD_N, SSD_D_INNER), per_b),
                  pl.BlockSpec((CONV_W, SSD_CONV_DIM), fixed), pl.BlockSpec((1, SSD_CONV_DIM), fixed),
                  pl.BlockSpec((1, LANES), fixed), pl.BlockSpec((1, LANES), fixed),
                  pl.BlockSpec((1, SSD_D_INNER), fixed)],
        out_specs=[pl.BlockSpec((None, tt, SSD_D_INNER), row),
                   pl.BlockSpec((None, CONV_W - 1, SSD_CONV_DIM), per_b),
                   pl.BlockSpec((None, SSD_N, SSD_D_INNER), per_b)],
        out_shape=[jax.ShapeDtypeStruct((bsz, t, SSD_D_INNER), F32),
                   jax.ShapeDtypeStruct((bsz, CONV_W - 1, SSD_CONV_DIM), F32),
                   jax.ShapeDtypeStruct((bsz, SSD_N, SSD_D_INNER), F32)],
        scratch_shapes=[pltpu.VMEM((8 + SSD_Q, SSD_CONV_DIM), F32), pltpu.VMEM((SSD_Q, LANES), F32),
                        pltpu.VMEM((SSD_N, SSD_D_INNER), F32)],
        compiler_params=_cp("parallel", "arbitrary"),
        name="ssd",
    )(xbc, dt, cbuf, h0t, cw, cb, dtb, a, dsk)


def _outproj_odd_kernel(*refs):
    y_ref, z_ref, x_ref, nw_ref, w_ref, g_ref, b_ref = refs[:7]
    o_ref, ot_ref = refs[-2:]
    gw = SSD_D_INNER // SSD_GROUPS
    acc = None
    for g in range(SSD_GROUPS):
        cs = slice(g * gw, (g + 1) * gw)
        v = y_ref[:, cs] * _silu(z_ref[:, cs])
        v = v * lax.rsqrt(jnp.mean(v * v, -1, keepdims=True) + RMS_EPS) * nw_ref[:, cs]
        part = jnp.dot(v.astype(BF16), w_ref[cs, :], preferred_element_type=F32)
        acc = part if acc is None else acc + part
    out = _ln_rows(ALPHA * x_ref[...] + acc, g_ref[...], b_ref[...])
    o_ref[...] = out
    _store_row_tiles(ot_ref, out)


def _router_kernel(x_ref, w_ref, b_ref, idx_ref, gate_ref, rank_ref, cnt_ref, carry, *, tm):
    i = pl.program_id(0)

    @pl.when(i == 0)
    def _():
        carry[...] = jnp.zeros(carry.shape, F32)

    logits = jnp.dot(x_ref[...].astype(BF16), w_ref[...].astype(BF16), preferred_element_type=F32) + b_ref[...]
    lane = lax.broadcasted_iota(I32, (tm, N_EXPERTS), 1)
    work = logits
    vals, hots = [], []
    for _ in range(TOP_K):
        m = jnp.max(work, axis=1, keepdims=True)
        ik = jnp.min(jnp.where(work == m, lane, N_EXPERTS), axis=1, keepdims=True)
        hot = lane == ik
        vals.append(m)
        hots.append(hot)
        work = jnp.where(hot, NEG_INF, work)
    es = [jnp.exp(v - vals[0]) for v in vals]
    den = es[0] + es[1] + es[2] + es[3]
    hot_all = (hots[0] | hots[1] | hots[2] | hots[3])
    r_i = lax.broadcasted_iota(I32, (tm, tm), 0)
    c_i = lax.broadcasted_iota(I32, (tm, tm), 1)
    before = (r_i > c_i).astype(BF16)
    pre = jnp.dot(before, hot_all.astype(BF16), preferred_element_type=F32) + carry[...]
    lane4 = lax.broadcasted_iota(I32, (tm, TOP_K), 1)
    idx = jnp.zeros((tm, TOP_K), I32)
    gate = jnp.zeros((tm, TOP_K), F32)
    rank = jnp.zeros((tm, TOP_K), I32)
    for k in range(TOP_K):
        ik = jnp.sum(jnp.where(hots[k], lane, 0), axis=1, keepdims=True)
        rk = jnp.sum(jnp.where(hots[k], pre, 0.0), axis=1, keepdims=True).astype(I32)
        idx = jnp.where(lane4 == k, ik, idx)
        rank = jnp.where(lane4 == k, rk, rank)
        gate = jnp.where(lane4 == k, es[k] / den, gate)
    idx_ref[...] = idx
    gate_ref[...] = gate
    rank_ref[...] = rank
    carry[...] = carry[...] + jnp.sum(hot_all.astype(F32), axis=0, keepdims=True)
    cnt_ref[...] = carry[...]


def _router(x_all, w, b):
    n = x_all.shape[0]
    tm = 384 if n % 384 == 0 else 128
    out4 = lambda dt: jax.ShapeDtypeStruct((n, TOP_K), dt)
    return pl.pallas_call(
        functools.partial(_router_kernel, tm=tm),
        grid=(n // tm,),
        in_specs=[pl.BlockSpec((tm, D_MODEL), lambda i: (i, 0)),
                  pl.BlockSpec((D_MODEL, N_EXPERTS), lambda i: (0, 0)),
                  pl.BlockSpec((1, N_EXPERTS), lambda i: (0, 0))],
        out_specs=[pl.BlockSpec((tm, TOP_K), lambda i: (i, 0))] * 3 + [pl.BlockSpec((1, N_EXPERTS), lambda i: (0, 0))],
        out_shape=[out4(I32), out4(F32), out4(I32), jax.ShapeDtypeStruct((1, N_EXPERTS), F32)],
        scratch_shapes=[pltpu.VMEM((1, N_EXPERTS), F32)],
        compiler_params=_cp("arbitrary"),
        name="router",
    )(x_all, w, b)


def _tile_copy(src_hbm, row8, dst, dst_row8, sem):
    return pltpu.make_async_copy(src_hbm.at[pl.ds(pl.multiple_of(row8, ROW_TILES), ROW_TILES), :],
                                 dst.at[pl.ds(pl.multiple_of(dst_row8, ROW_TILES), ROW_TILES), :], sem)


def _dispatch_kernel(pos_ref, xt_ref, xs_init_hbm, xs_hbm, sem, *, tm):
    del xs_init_hbm
    i = pl.program_id(0)

    def issue(t, _):
        src = xt_ref.at[pl.ds(pl.multiple_of(t * ROW_TILES, ROW_TILES), ROW_TILES), :]
        for k in range(TOP_K):
            slot8 = pl.multiple_of(pos_ref[(i * tm + t) * TOP_K + k], ROW_TILES)
            pltpu.make_async_copy(src, xs_hbm.at[pl.ds(slot8, ROW_TILES), :], sem).start(priority=k % 2)
        return 0

    lax.fori_loop(0, tm, issue, 0, unroll=4)
    for _ in range(TOP_K):
        pltpu.make_async_copy(xt_ref, xs_hbm.at[pl.ds(0, tm * ROW_TILES), :], sem).wait()


def _dispatch(pos8_flat, xt_all, n_slots):
    n = xt_all.shape[0] // ROW_TILES
    tm = 128
    grid_spec = pltpu.PrefetchScalarGridSpec(
        num_scalar_prefetch=1,
        grid=(n // tm,),
        in_specs=[pl.BlockSpec((tm * ROW_TILES, LANES), lambda i, pos: (i, 0)),
                  pl.BlockSpec(memory_space=pl.ANY)],
        out_specs=pl.BlockSpec(memory_space=pl.ANY),
        scratch_shapes=[pltpu.SemaphoreType.DMA],
    )
    return pl.pallas_call(
        functools.partial(_dispatch_kernel, tm=tm),
        grid_spec=grid_spec,
        out_shape=jax.ShapeDtypeStruct((n_slots * ROW_TILES, LANES), F32),
        input_output_aliases={2: 0},
        compiler_params=_cp("arbitrary"),
        name="moe_dispatch",
    )(pos8_flat, xt_all, jnp.zeros((n_slots * ROW_TILES, LANES), F32))


def _ffn_kernel(te_ref, nv_ref, xs_ref, w1_ref, b1_ref, w2_ref, b2_ref, ot_ref, w1b, w2i, w2b, *, rows):
    i = pl.program_id(0)
    nvalid = nv_ref[0]
    half = D_FF // 2

    @pl.when((i < nvalid) & ((i == 0) | (te_ref[i] != te_ref[jnp.maximum(i - 1, 0)])))
    def _():
        w1b[...] = w1_ref[...].astype(BF16)
        for c in range(ROW_TILES):
            cs = slice(c * LANES, (c + 1) * LANES)
            w2i[c, pl.ds(0, half, stride=2), :] = w2_ref[0:half, cs]
            w2i[c, pl.ds(1, half, stride=2), :] = w2_ref[half:D_FF, cs]
            w2b[:, cs] = w2i[c].astype(BF16)

    @pl.when(i < nvalid)
    def _():
        xb = _load_row_tiles(xs_ref, rows).astype(BF16)
        h = jnp.dot(xb, w1b[...], preferred_element_type=F32) + b1_ref[...]
        even = (lax.broadcasted_iota(I32, (rows, LANES), 1) % 2) == 0
        parts = []
        for c in range(D_FF // LANES):
            ha = h[:, c * LANES:(c + 1) * LANES]
            hb = h[:, D_FF + c * LANES:D_FF + (c + 1) * LANES]
            glu = jnp.minimum(jnp.where(even, ha, pltpu.roll(hb, 1, 1)), SWIGLU_LIMIT)
            lin = jnp.clip(jnp.where(even, pltpu.roll(ha, LANES - 1, 1), hb), -SWIGLU_LIMIT, SWIGLU_LIMIT)
            parts.append((glu * _sigmoid(SWIGLU_ALPHA * glu) * (lin + 1.0)).astype(BF16))
        act = jnp.concatenate(parts, axis=1)
        _store_row_tiles(ot_ref, jnp.dot(act, w2b[...], preferred_element_type=F32) + b2_ref[...])

    @pl.when(i >= nvalid)
    def _():
        ot_ref[...] = jnp.zeros(ot_ref.shape, F32)


def _ffn(xs_t, tile_expert, nvalid, layer, w1, b1, w2, b2, ntiles):
    rows = MOE_ROWS
    wspec = lambda shape: pl.BlockSpec((None, None) + shape, lambda i, te, nv: (layer, te[i], 0, 0))
    grid_spec = pltpu.PrefetchScalarGridSpec(
        num_scalar_prefetch=2,
        grid=(ntiles,),
        in_specs=[pl.BlockSpec((rows * ROW_TILES, LANES), lambda i, te, nv: (i, 0)),
                  wspec((D_MODEL, 2 * D_FF)), wspec((1, 2 * D_FF)),
                  wspec((D_FF, D_MODEL)), wspec((1, D_MODEL))],
        out_specs=pl.BlockSpec((rows * ROW_TILES, LANES), lambda i, te, nv: (i, 0)),
        scratch_shapes=[pltpu.VMEM((D_MODEL, 2 * D_FF), BF16), pltpu.VMEM((ROW_TILES, D_FF, LANES), F32),
                        pltpu.VMEM((D_FF, D_MODEL), BF16)],
    )
    return pl.pallas_call(
        functools.partial(_ffn_kernel, rows=rows),
        grid_spec=grid_spec,
        out_shape=jax.ShapeDtypeStruct((ntiles * rows * ROW_TILES, LANES), F32),
        compiler_params=_cp("arbitrary"),
        name="moe_ffn",
    )(tile_expert, nvalid, xs_t, w1, b1, w2, b2)


def _combine_kernel(pos_ref, ot_hbm, gate_ref, x_ref, g_ref, b_ref, out_ref, buf, sem, *, tm):
    i = pl.program_id(0)
    n = pl.num_programs(0)
    slot = i % 2

    def gather(tile, s):
        def issue(t, _):
            for k in range(TOP_K):
                _tile_copy(ot_hbm, pos_ref[(tile * tm + t) * TOP_K + k], buf.at[s, k], t * ROW_TILES,
                           sem.at[s]).start(priority=k % 2)
            return 0
        lax.fori_loop(0, tm, issue, 0, unroll=4)

    @pl.when(i == 0)
    def _():
        gather(0, 0)

    @pl.when(i + 1 < n)
    def _():
        gather(i + 1, 1 - slot)

    for k in range(TOP_K):
        pltpu.make_async_copy(ot_hbm.at[pl.ds(0, tm * ROW_TILES), :], buf.at[slot, k], sem.at[slot]).wait()
    f = None
    for k in range(TOP_K):
        part = gate_ref[:, k:k + 1] * _load_row_tiles(buf.at[slot, k], tm)
        f = part if f is None else f + part
    out_ref[...] = _ln_rows(ALPHA * x_ref[...] + f, g_ref[...], b_ref[...])


def _combine(pos_flat, ot_sorted, gate, x_all, g, b):
    n = x_all.shape[0]
    tm = 128
    grid_spec = pltpu.PrefetchScalarGridSpec(
        num_scalar_prefetch=1,
        grid=(n // tm,),
        in_specs=[pl.BlockSpec(memory_space=pl.ANY),
                  pl.BlockSpec((tm, TOP_K), lambda i, pos: (i, 0)),
                  pl.BlockSpec((tm, D_MODEL), lambda i, pos: (i, 0)),
                  pl.BlockSpec((1, D_MODEL), lambda i, pos: (0, 0)),
                  pl.BlockSpec((1, D_MODEL), lambda i, pos: (0, 0))],
        out_specs=pl.BlockSpec((tm, D_MODEL), lambda i, pos: (i, 0)),
        scratch_shapes=[pltpu.VMEM((2, TOP_K, tm * ROW_TILES, LANES), F32), pltpu.SemaphoreType.DMA((2,))],
    )
    return pl.pallas_call(
        functools.partial(_combine_kernel, tm=tm),
        grid_spec=grid_spec,
        out_shape=jax.ShapeDtypeStruct((n, D_MODEL), F32),
        compiler_params=_cp("arbitrary"),
        name="moe_combine",
    )(pos_flat, ot_sorted, gate, x_all, g, b)


def _moe(x_all, xt_all, layer, w_r, b_r, w1, b1, w2, b2, g, b):
    n = x_all.shape[0]
    rows = MOE_ROWS
    ntiles = -(-(n * TOP_K + N_EXPERTS * (rows - 1)) // rows)
    idx, gate, rank, cnt = _router(x_all, w_r, b_r.reshape(1, N_EXPERTS))
    counts = cnt[0].astype(I32)
    padded = ((counts + rows - 1) // rows) * rows
    ends = jnp.cumsum(padded)
    starts = ends - padded
    experts = jnp.arange(N_EXPERTS, dtype=I32)
    pos = rank + jnp.sum(jnp.where(idx[:, :, None] == experts, starts, 0), axis=-1)
    tile_starts = jnp.arange(ntiles, dtype=I32) * rows
    tile_expert = jnp.minimum(jnp.sum((tile_starts[:, None] >= ends[None, :]).astype(I32), axis=1), N_EXPERTS - 1)
    nvalid = (ends[-1:] // rows).astype(I32)
    pos8 = pos.reshape(-1) * ROW_TILES
    xs_t = _dispatch(pos8, xt_all, ntiles * rows)
    ot_sorted = _ffn(xs_t, tile_expert, nvalid, layer, w1, b1[:, :, None, :], w2, b2[:, :, None, :], ntiles)
    return _combine(pos8, ot_sorted, gate, x_all, g.reshape(1, -1), b.reshape(1, -1))


def _block_diag(w):
    nb, bw, _ = w.shape
    eye = jnp.eye(nb, dtype=w.dtype)
    return (eye[:, None, :, None] * w[:, :, None, :]).reshape(nb * bw, nb * bw)


def _mix_even(x_arr, out_row0, n_total, bsz, t, rg_conv, rg_h, cache, p, prev):
    n = bsz * t
    xa, ga, q, kv = _inproj_even(x_arr, 0, n, p["w_in"])
    ya, conv_new, h_last = _rglru(
        xa.reshape(bsz, t, RG_WIDTH), ga.reshape(bsz, t, RG_WIDTH), rg_conv, rg_h.reshape(bsz, 1, RG_WIDTH),
        p["rg_conv_w"], p["rg_conv_b"], p["rg_wg"], p["rg_bg"], p["rg_c"])
    q3 = q.reshape(bsz, t, B_WIDTH)
    kv3 = kv.reshape(bsz, t, 2 * B_WIDTH)
    if cache is None:
        yb = _attn_prompt(q3, kv3, p["rel_bias"])
    else:
        yb = _attn_sample(q3, cache, kv3, p["rel_bias"])
    x1 = _token_call(_outproj_even_kernel, "outproj_even", n, 0, n_total, out_row0,
                     [ya.reshape(n, RG_WIDTH), yb.reshape(n, B_WIDTH)], x_arr,
                     [p["w_out"], p["ln_g0"], p["ln_b0"]], prev)
    return x1, conv_new, h_last.reshape(bsz, RG_WIDTH), kv3


def _mix_odd(x_all, row0, bsz, t, ssd_conv, ssd_h, p, prev):
    n = bsz * t
    z, xbc, dt = _inproj_odd(x_all, row0, n, p["ssd_w_main"], p["ssd_w_dt"])
    h0t = jnp.transpose(ssd_h.astype(F32), (0, 3, 1, 2)).reshape(bsz, SSD_N, SSD_D_INNER)
    y, conv_new, h_t = _ssd(xbc.reshape(bsz, t, SSD_CONV_DIM), dt.reshape(bsz, t, LANES), ssd_conv, h0t,
                            p["ssd_conv_w"], p["ssd_conv_b"], p["ssd_dtb"], p["ssd_a"], p["ssd_dsk"])
    h_last = jnp.transpose(h_t.reshape(bsz, SSD_N, SSD_HEADS, SSD_P), (0, 2, 3, 1))
    x1 = _token_call(_outproj_odd_kernel, "outproj_odd", n, row0, x_all.shape[0], row0,
                     [y.reshape(n, SSD_D_INNER), z], x_all,
                     [p["ssd_norm_w"], p["ssd_w_out"], p["ln_g2"], p["ln_b2"]], prev)
    return x1, conv_new, h_last


def _mixer_params(rel_bias, w_in_mix, rg_conv_w, rg_conv_b, rg_w_a, rg_b_a, rg_w_i, rg_b_i, rg_lambda, w_out_mix,
                  ssd_w_in, ssd_conv_w, ssd_conv_b, ssd_dt_bias, ssd_a_log, ssd_d, ssd_norm_w, ssd_w_out, ln_g, ln_b):
    row2 = lambda v: v.reshape(1, -1).astype(F32)
    pad_heads = lambda v: jnp.pad(v.astype(F32), (0, LANES - SSD_HEADS)).reshape(1, LANES)
    return dict(
        rel_bias=rel_bias,
        w_in=w_in_mix[0].astype(BF16),
        rg_conv_w=rg_conv_w[0].astype(F32), rg_conv_b=row2(rg_conv_b[0]),
        rg_wg=jnp.concatenate([_block_diag(rg_w_a[0]), _block_diag(rg_w_i[0])], axis=1).astype(BF16),
        rg_bg=jnp.concatenate([rg_b_a[0].reshape(1, -1), rg_b_i[0].reshape(1, -1)], axis=1).astype(F32),
        rg_c=row2(-RG_C * jax.nn.softplus(-rg_lambda[0].astype(F32))),
        w_out=w_out_mix[0].astype(BF16),
        ln_g0=row2(ln_g[0, 0]), ln_b0=row2(ln_b[0, 0]),
        ssd_w_main=ssd_w_in[0][:, :SSD_MAIN].astype(BF16),
        ssd_w_dt=jnp.pad(ssd_w_in[0][:, SSD_MAIN:], ((0, 0), (0, LANES - SSD_HEADS))).astype(BF16),
        ssd_conv_w=ssd_conv_w[0].astype(F32), ssd_conv_b=row2(ssd_conv_b[0]),
        ssd_dtb=pad_heads(ssd_dt_bias[0]), ssd_a=pad_heads(-jnp.exp(ssd_a_log[0].astype(F32))),
        ssd_dsk=row2(jnp.repeat(ssd_d[0].astype(F32), SSD_P)),
        ssd_norm_w=row2(ssd_norm_w[0]), ssd_w_out=ssd_w_out[0].astype(BF16),
        ln_g2=row2(ln_g[1, 0]), ln_b2=row2(ln_b[1, 0]),
    )


def kernel(x_prompt, x_sample, state_rglru_conv, state_rglru_h, cache_swa_kv, state_ssd_conv, state_ssd_h,
           rel_bias, w_in_mix, rg_conv_w, rg_conv_b, rg_w_a, rg_b_a, rg_w_i, rg_b_i, rg_lambda, w_out_mix,
           ssd_w_in, ssd_conv_w, ssd_conv_b, ssd_dt_bias, ssd_a_log, ssd_d, ssd_norm_w, ssd_w_out,
           ln_g, ln_b, router_w, router_b, exp_w1, exp_b1, exp_w2, exp_b2):
    bp, tp, _ = x_prompt.shape
    bs, ts, _ = x_sample.shape
    n_p, n_s = bp * tp, bs * ts
    n_total = n_p + n_s
    p = _mixer_params(rel_bias, w_in_mix, rg_conv_w, rg_conv_b, rg_w_a, rg_b_a, rg_w_i, rg_b_i, rg_lambda, w_out_mix,
                      ssd_w_in, ssd_conv_w, ssd_conv_b, ssd_dt_bias, ssd_a_log, ssd_d, ssd_norm_w, ssd_w_out, ln_g, ln_b)

    xp = x_prompt.reshape(n_p, D_MODEL)
    xs = x_sample.reshape(n_s, D_MODEL)
    zeros = lambda *s: jnp.zeros(s, F32)
    x1, p_rg_conv, p_rg_h, p_kv = _mix_even(xp, 0, n_total, bp, tp, zeros(bp, CONV_W - 1, RG_WIDTH),
                                            zeros(bp, RG_WIDTH), None, p, None)
    cache = cache_swa_kv[0].reshape(bs, cache_swa_kv.shape[2], 2 * B_WIDTH)
    x1, s_rg_conv, s_rg_h, s_kv = _mix_even(xs, n_p, n_total, bs, ts, state_rglru_conv[0], state_rglru_h[0],
                                            cache, p, x1)
    x2 = _moe(x1[0], x1[1], 0, router_w[0].astype(F32), router_b[0].astype(F32), exp_w1, exp_b1, exp_w2, exp_b2,
              ln_g[0, 1], ln_b[0, 1])

    x3, p_ssd_conv, p_ssd_h = _mix_odd(x2, 0, bp, tp, zeros(bp, CONV_W - 1, SSD_CONV_DIM),
                                       zeros(bp, SSD_HEADS, SSD_P, SSD_N), p, None)
    x3, s_ssd_conv, s_ssd_h = _mix_odd(x2, n_p, bs, ts, state_ssd_conv[0], state_ssd_h[0], p, x3)
    x4 = _moe(x3[0], x3[1], 1, router_w[1].astype(F32), router_b[1].astype(F32), exp_w1, exp_b1, exp_w2, exp_b2,
              ln_g[1, 1], ln_b[1, 1])

    kv_shape = lambda b, t: (1, b, t, 2, B_HEADS, B_HD)
    return (x4[:n_p].reshape(bp, tp, D_MODEL), x4[n_p:].reshape(bs, ts, D_MODEL),
            p_rg_conv[None], p_rg_h[None], p_kv[:, -min(MAX_WINDOW, tp):].reshape(kv_shape(bp, min(MAX_WINDOW, tp))),
            p_ssd_conv[None], p_ssd_h[None],
            s_rg_conv[None], s_rg_h[None], s_kv.reshape(kv_shape(bs, ts)),
            s_ssd_conv[None], s_ssd_h[None])
```

```python
import functools
import math

import numpy as np
import jax
import jax.numpy as jnp
from jax import lax
from jax.experimental import pallas as pl
from jax.experimental.pallas import tpu as pltpu

F32 = jnp.float32
BF16 = jnp.bfloat16
I32 = jnp.int32

D_MODEL = 1024
DEPTH = 2
ALPHA = (2 * DEPTH) ** 0.25
LN_EPS = 1e-5
RMS_EPS = 1e-5
CONV_W = 4
RG_BLOCKS = 12
RG_BW = 64
RG_WIDTH = RG_BLOCKS * RG_BW
RG_C = 8.0
B_HEADS = 12
B_HD = 64
B_WIDTH = B_HEADS * B_HD
PATTERNS = ((128, 1), (512, 4), (2048, 16))
MAX_WINDOW = 2048
N_BUCKETS = 32
MAX_DISTANCE = MAX_WINDOW
EVEN_IN = 2 * RG_WIDTH + 3 * B_WIDTH
SSD_D_INNER = 2 * D_MODEL
SSD_P = 64
SSD_HEADS = SSD_D_INNER // SSD_P
SSD_N = 128
SSD_GROUPS = 4
SSD_GN = SSD_GROUPS * SSD_N
SSD_CONV_DIM = SSD_D_INNER + 2 * SSD_GN
SSD_MAIN = SSD_D_INNER + SSD_CONV_DIM
N_EXPERTS = 32
TOP_K = 4
D_FF = D_MODEL
SWIGLU_LIMIT = 7.0
SWIGLU_ALPHA = 1.702

LANES = 128
VMEM_LIMIT = 56 * 1024 * 1024

SSD_Q = 128
ATT_BLK = 256
ATT_CHUNK = 64
MOE_ROWS = 256
NEG_INF = float("-inf")


ROW_TILES = D_MODEL // LANES


def _cp(*sem):
    return pltpu.CompilerParams(dimension_semantics=sem, vmem_limit_bytes=VMEM_LIMIT)


def _store_row_tiles(ref, val):
    rows = val.shape[0]
    for s in range(ROW_TILES):
        ref[pl.ds(s, rows, stride=ROW_TILES), :] = val[:, s * LANES:(s + 1) * LANES]


def _load_row_tiles(ref, rows):
    return jnp.concatenate([ref[pl.ds(s, rows, stride=ROW_TILES), :] for s in range(ROW_TILES)], axis=1)


def _ln_rows(v, g, b):
    mu = jnp.mean(v, -1, keepdims=True)
    xc = v - mu
    var = jnp.mean(xc * xc, -1, keepdims=True)
    return xc * lax.rsqrt(var + LN_EPS) * g + b


def _bf16_round(x):
    return x.astype(BF16).astype(F32)


def _sigmoid(x):
    return 1.0 / (1.0 + jnp.exp(-x))


def _silu(x):
    return x * _sigmoid(x)


def _softplus(x):
    return jnp.maximum(x, 0.0) + jnp.log1p(jnp.exp(-jnp.abs(x)))


def _gelu_tanh(x):
    return 0.5 * x * (1.0 + jnp.tanh(math.sqrt(2.0 / math.pi) * (x + 0.044715 * (x * x * x))))


def _row_tile(n):
    return min(n, 256)


def _inproj_even_kernel(x_ref, w_ref, xa_ref, ga_ref, q_ref, kv_ref):
    xb = x_ref[...].astype(BF16)
    c0, c1, c2 = RG_WIDTH, 2 * RG_WIDTH, 2 * RG_WIDTH + B_WIDTH
    xa_ref[...] = jnp.dot(xb, w_ref[:, 0:c0], preferred_element_type=F32)
    ga_ref[...] = jnp.dot(xb, w_ref[:, c0:c1], preferred_element_type=F32)
    q_ref[...] = jnp.dot(xb, w_ref[:, c1:c2], preferred_element_type=F32)
    kv_ref[...] = jnp.dot(xb, w_ref[:, c2:EVEN_IN], preferred_element_type=F32)


def _inproj_even(x_all, row0, n, w):
    tm = _row_tile(n)
    off = row0 // tm
    widths = (RG_WIDTH, RG_WIDTH, B_WIDTH, 2 * B_WIDTH)
    return pl.pallas_call(
        _inproj_even_kernel,
        grid=(n // tm,),
        in_specs=[pl.BlockSpec((tm, D_MODEL), lambda i: (i + off, 0)),
                  pl.BlockSpec((D_MODEL, EVEN_IN), lambda i: (0, 0))],
        out_specs=[pl.BlockSpec((tm, c), lambda i: (i, 0)) for c in widths],
        out_shape=[jax.ShapeDtypeStruct((n, c), F32) for c in widths],
        compiler_params=_cp("parallel"),
        name="inproj_even",
    )(x_all, w)


def _rglru_kernel(xa_ref, ga_ref, cbuf_ref, h0_ref, cw_ref, cb_ref, wg_ref, bg_ref, c_ref,
                  ya_ref, cnew_ref, hlast_ref, xx, a_s, u_s, hc, *, tt):
    j = pl.program_id(1)
    hist = CONV_W - 1
    base = 8

    @pl.when(j == 0)
    def _():
        xx[base - hist:base, :] = cbuf_ref[...]
        hc[...] = h0_ref[...]

    @pl.when(j > 0)
    def _():
        xx[base - hist:base, :] = xx[base + tt - hist:base + tt, :]

    xx[base:base + tt, :] = xa_ref[...]
    conv = cb_ref[...]
    for k in range(CONV_W):
        conv = conv + cw_ref[k:k + 1, :] * _bf16_round(xx[base - hist + k:base - hist + k + tt, :])
    gates = jnp.dot(conv.astype(BF16), wg_ref[...], preferred_element_type=F32) + bg_ref[...]
    r = _sigmoid(gates[:, :RG_WIDTH])
    ig = _sigmoid(gates[:, RG_WIDTH:])
    log_a = c_ref[...] * r
    a = jnp.exp(log_a)
    a_s[...] = a
    u_s[...] = jnp.sqrt(-jnp.tanh(log_a) * (a * a + 1.0)) * (ig * conv)

    def step(t, h):
        h = a_s[pl.ds(t, 1), :] * h + u_s[pl.ds(t, 1), :]
        u_s[pl.ds(t, 1), :] = h
        return h

    h = lax.fori_loop(0, tt, step, hc[...], unroll=min(tt, 8))
    hc[...] = h
    ya_ref[...] = _gelu_tanh(ga_ref[...]) * u_s[...]

    @pl.when(j == pl.num_programs(1) - 1)
    def _():
        hlast_ref[...] = h
        cnew_ref[...] = xx[base + tt - hist:base + tt, :]


def _rglru(xa, ga, cbuf, h0, cw, cb, wg, bg, cvec):
    bsz, t, c = xa.shape
    tt = min(t, 256)
    row = lambda b, j: (b, j, 0)
    per_b = lambda b, j: (b, 0, 0)
    fixed = lambda b, j: (0, 0)
    return pl.pallas_call(
        functools.partial(_rglru_kernel, tt=tt),
        grid=(bsz, t // tt),
        in_specs=[pl.BlockSpec((None, tt, c), row), pl.BlockSpec((None, tt, c), row),
                  pl.BlockSpec((None, CONV_W - 1, c), per_b), pl.BlockSpec((None, 1, c), per_b),
                  pl.BlockSpec((CONV_W, c), fixed), pl.BlockSpec((1, c), fixed),
                  pl.BlockSpec((c, 2 * c), fixed), pl.BlockSpec((1, 2 * c), fixed),
                  pl.BlockSpec((1, c), fixed)],
        out_specs=[pl.BlockSpec((None, tt, c), row), pl.BlockSpec((None, CONV_W - 1, c), per_b),
                   pl.BlockSpec((None, 1, c), per_b)],
        out_shape=[jax.ShapeDtypeStruct((bsz, t, c), F32),
                   jax.ShapeDtypeStruct((bsz, CONV_W - 1, c), F32),
                   jax.ShapeDtypeStruct((bsz, 1, c), F32)],
        scratch_shapes=[pltpu.VMEM((8 + tt, c), F32), pltpu.VMEM((tt, c), F32),
                        pltpu.VMEM((tt, c), F32), pltpu.VMEM((1, c), F32)],
        compiler_params=_cp("parallel", "arbitrary"),
        name="rglru",
    )(xa, ga, cbuf, h0, cw, cb, wg, bg, cvec)


def _t5_bucket(dist):
    max_exact = N_BUCKETS // 2
    safe = np.maximum(dist, 1)
    large = max_exact + (np.log(safe / max_exact) / np.log(MAX_DISTANCE / max_exact)
                         * (N_BUCKETS - max_exact)).astype(np.int32)
    return np.where(dist < max_exact, dist, np.minimum(large, N_BUCKETS - 1)).astype(np.int32)


def _distance_bias(rel_bias, max_d):
    d = np.arange(max_d + 1)
    mult = np.zeros(max_d + 1, np.float64)
    for w, dil in PATTERNS:
        mult += ((d % dil == 0) & (d <= w)).astype(np.float64)
    logm = np.where(mult > 0, np.log(np.maximum(mult, 1.0)), -np.inf).astype(np.float32)
    vals = rel_bias.astype(F32)[_t5_bucket(d)].T + jnp.asarray(logm)[None, :]
    return jnp.concatenate([vals, jnp.full((vals.shape[0], 1), NEG_INF, F32)], axis=1)


def _attn_prompt_kernel(q_ref, k_ref, v_ref, rv_ref, o_ref, tab_ref, k_s, v_s, *, blk, nq):
    qi = pl.program_id(2)

    @pl.when((pl.program_id(1) == 0) & (qi == 0))
    def _():
        for h in range(2):
            for d in range(nq):
                win = rv_ref[h:h + 1, (nq - 1 - d) * blk:(nq + 1 - d) * blk]
                rolled = pltpu.roll(jnp.broadcast_to(win, (blk, 2 * blk)), 0, 1, stride=1, stride_axis=0)
                tab_ref[h, d] = rolled[:, blk:]

    lo = lax.broadcasted_iota(I32, (blk, LANES), 1) < B_HD

    @pl.when(qi == 0)
    def _():
        k_s[...] = k_ref[...].astype(BF16)
        v_s[...] = v_ref[...].astype(BF16)

    q2 = q_ref[...] * (B_HD ** -0.5)
    qh = (jnp.where(lo, q2, 0.0).astype(BF16), jnp.where(lo, 0.0, q2).astype(BF16))
    nt = (((1,), (1,)), ((), ()))
    ch = ATT_CHUNK

    def scores(d):
        k = k_s[pl.ds(pl.multiple_of((qi - d) * blk, blk), blk), :]
        return tuple(lax.dot_general(qh[h], k, nt, preferred_element_type=F32) for h in range(2))

    def body(d, carry):
        m0, l0, m1, l1, acc, s0, s1 = carry
        s_next = scores(jnp.minimum(d + 1, qi))
        v = v_s[pl.ds(pl.multiple_of((qi - d) * blk, blk), blk), :]
        new = []
        for h, (m, l, s) in enumerate(((m0, l0, s0), (m1, l1, s1))):
            mns, lns, als, ps = [], [], [], []
            for c in range(blk // ch):
                rs = slice(c * ch, (c + 1) * ch)
                sc = s[rs] + tab_ref[h, d, rs, :]
                mn = jnp.maximum(m[rs], jnp.broadcast_to(jnp.max(sc, axis=1, keepdims=True), (ch, LANES)))
                p = jnp.concatenate([jnp.exp(sc[:, j * LANES:(j + 1) * LANES] - mn) for j in range(blk // LANES)], axis=1)
                al = jnp.exp(m[rs] - mn)
                mns.append(mn)
                als.append(al)
                lns.append(al * l[rs] + jnp.broadcast_to(jnp.sum(p, axis=1, keepdims=True), (ch, LANES)))
                ps.append(p.astype(BF16))
            cat = lambda xs: jnp.concatenate(xs, axis=0)
            new.append((cat(mns), cat(lns), cat(als), jnp.dot(cat(ps), v, preferred_element_type=F32)))
        (m0, l0, a0, pv0), (m1, l1, a1, pv1) = new
        acc = jnp.where(lo, a0, a1) * acc + jnp.where(lo, pv0, pv1)
        return (m0, l0, m1, l1, acc) + s_next

    minf = jnp.full((blk, LANES), NEG_INF, F32)
    zero = jnp.zeros((blk, LANES), F32)
    out = lax.fori_loop(0, qi + 1, body, (minf, zero, minf, zero, zero) + scores(0))
    m0, l0, m1, l1, acc = out[:5]
    o_ref[...] = acc / jnp.where(lo, l0, l1)


def _attn_prompt(q, kv, rel_bias):
    bsz, t, _ = q.shape
    blk = min(ATT_BLK, t)
    nq = t // blk
    npair = B_HEADS // 2
    vals = _distance_bias(rel_bias, t)[:, :t + 1]
    rv = jnp.concatenate([jnp.full((B_HEADS, blk - 1), NEG_INF, F32), vals], axis=1)[:, ::-1]
    rv = rv.reshape(npair, 2, t + blk)
    return pl.pallas_call(
        functools.partial(_attn_prompt_kernel, blk=blk, nq=nq),
        grid=(npair, bsz, nq),
        in_specs=[pl.BlockSpec((None, blk, LANES), lambda p, b, i: (b, i, p)),
                  pl.BlockSpec((None, t, LANES), lambda p, b, i: (b, 0, p)),
                  pl.BlockSpec((None, t, LANES), lambda p, b, i: (b, 0, npair + p)),
                  pl.BlockSpec((None, 2, t + blk), lambda p, b, i: (p, 0, 0))],
        out_specs=pl.BlockSpec((None, blk, LANES), lambda p, b, i: (b, i, p)),
        out_shape=jax.ShapeDtypeStruct((bsz, t, B_WIDTH), F32),
        scratch_shapes=[pltpu.VMEM((2, nq, blk, blk), F32), pltpu.VMEM((t, LANES), BF16),
                        pltpu.VMEM((t, LANES), BF16)],
        compiler_params=_cp("arbitrary", "arbitrary", "arbitrary"),
        name="attn_prompt",
    )(q, kv, kv, rv)


ATT_S_ROWS = 16


def _attn_sample_kernel(qbd_ref, kc_ref, vc_ref, kn_ref, vn_ref, tabc_ref, tabn_ref, mask_ref, o_ref,
                        m_s, l_s, acc_s, *, t_new):
    c = pl.program_id(1)
    rows = t_new * ATT_S_ROWS
    nt = (((1,), (1,)), ((), ()))

    @pl.when(c == 0)
    def _():
        m_s[...] = jnp.full((rows, 1), NEG_INF, F32)
        l_s[...] = jnp.zeros((rows, 1), F32)
        acc_s[...] = jnp.zeros((rows, B_WIDTH), F32)

    qf = qbd_ref[...] * (B_HD ** -0.5)
    qb = qf.astype(BF16)
    s = lax.dot_general(qb, kc_ref[...].astype(BF16), nt, preferred_element_type=F32) + tabc_ref[...]
    m = m_s[...]
    mn = jnp.maximum(m, jnp.max(s, axis=1, keepdims=True))
    last = c == pl.num_programs(1) - 1

    sn = [jnp.sum(qf * kn_ref[j:j + 1, :], axis=1, keepdims=True) + tabn_ref[:, j:j + 1] for j in range(t_new)]
    sn = [jnp.where(last, x, NEG_INF) for x in sn]
    for x in sn:
        mn = jnp.maximum(mn, x)
    safe = jnp.where(mn == NEG_INF, 0.0, mn)
    p = jnp.exp(s - safe)
    al = jnp.exp(m - safe)
    l = al * l_s[...] + jnp.sum(p, axis=1, keepdims=True)
    acc = al * acc_s[...] + jnp.dot(p.astype(BF16), vc_ref[...].astype(BF16), preferred_element_type=F32)
    for j, x in enumerate(sn):
        pj = jnp.exp(x - safe)
        l = l + pj
        acc = acc + pj * vn_ref[j:j + 1, :]
    m_s[...] = mn
    l_s[...] = l
    acc_s[...] = acc

    @pl.when(last)
    def _():
        om = (acc / l) * mask_ref[...]
        for t in range(t_new):
            o_ref[t:t + 1, :] = jnp.sum(om[t * ATT_S_ROWS:(t + 1) * ATT_S_ROWS, :], axis=0, keepdims=True)


def _attn_sample(q, cache, kv_new, rel_bias):
    bsz, t_new, _ = q.shape
    w = cache.shape[1]
    rows = t_new * ATT_S_ROWS
    vals = _distance_bias(rel_bias, MAX_WINDOW)
    vals = jnp.concatenate([vals, jnp.zeros((ATT_S_ROWS - B_HEADS, vals.shape[1]), F32)], axis=0)
    tq = np.repeat(np.arange(t_new), ATT_S_ROWS)
    hq = np.tile(np.arange(ATT_S_ROWS), t_new)
    ext = jnp.concatenate([vals[:, :MAX_WINDOW + 1], jnp.full((ATT_S_ROWS, w + t_new - MAX_WINDOW), NEG_INF, F32)], axis=1)
    ext = ext.at[B_HEADS:, :].set(0.0)
    tabc = jnp.stack([ext[:, t + 1:w + t + 1][:, ::-1] for t in range(t_new)]).reshape(rows, w)
    dist_n = tq[:, None] - np.arange(t_new)[None, :]
    dist_n = np.where(dist_n < 0, MAX_WINDOW + 1, dist_n)
    tabn = vals[jnp.asarray(hq)[:, None], jnp.asarray(dist_n)]
    mask = (hq[:, None] == (np.arange(B_WIDTH)[None, :] // B_HD)).astype(np.float32)
    qbd = (q[:, :, None, :] * jnp.asarray(mask.reshape(t_new, ATT_S_ROWS, B_WIDTH))[None]).reshape(bsz, rows, B_WIDTH)
    kc = 1024
    return pl.pallas_call(
        functools.partial(_attn_sample_kernel, t_new=t_new),
        grid=(bsz, w // kc),
        in_specs=[pl.BlockSpec((None, rows, B_WIDTH), lambda b, c: (b, 0, 0)),
                  pl.BlockSpec((None, kc, B_WIDTH), lambda b, c: (b, c, 0)),
                  pl.BlockSpec((None, kc, B_WIDTH), lambda b, c: (b, c, 1)),
                  pl.BlockSpec((None, t_new, B_WIDTH), lambda b, c: (b, 0, 0)),
                  pl.BlockSpec((None, t_new, B_WIDTH), lambda b, c: (b, 0, 1)),
                  pl.BlockSpec((rows, kc), lambda b, c: (0, c)),
                  pl.BlockSpec((rows, t_new), lambda b, c: (0, 0)),
                  pl.BlockSpec((rows, B_WIDTH), lambda b, c: (0, 0))],
        out_specs=pl.BlockSpec((None, t_new, B_WIDTH), lambda b, c: (b, 0, 0)),
        out_shape=jax.ShapeDtypeStruct((bsz, t_new, B_WIDTH), F32),
        scratch_shapes=[pltpu.VMEM((rows, 1), F32), pltpu.VMEM((rows, 1), F32), pltpu.VMEM((rows, B_WIDTH), F32)],
        compiler_params=_cp("parallel", "arbitrary"),
        name="attn_sample",
    )(qbd, cache, cache, kv_new, kv_new, tabc, tabn, jnp.asarray(mask))


def _outproj_even_kernel(*refs):
    ya_ref, yb_ref, x_ref, w_ref, g_ref, b_ref = refs[:6]
    o_ref, ot_ref = refs[-2:]
    y = jnp.dot(ya_ref[...].astype(BF16), w_ref[0:RG_WIDTH, :], preferred_element_type=F32)
    y = y + jnp.dot(yb_ref[...].astype(BF16), w_ref[RG_WIDTH:, :], preferred_element_type=F32)
    out = _ln_rows(ALPHA * x_ref[...] + y, g_ref[...], b_ref[...])
    o_ref[...] = out
    _store_row_tiles(ot_ref, out)


def _token_call(kernel_fn, name, n, x_off_rows, out_rows, out_off_rows, row_inputs, x_arr, fixed_inputs, prev):
    tm = _row_tile(n)
    nt = n // tm
    xo, oo = x_off_rows // tm, out_off_rows // tm
    tail = prev is None and out_rows > out_off_rows + n
    if tail:
        assert out_off_rows == 0 and out_rows - n <= tm
        body = kernel_fn

        def kernel_fn(*refs):
            step = pl.program_id(0)
            pl.when(step < nt)(lambda: body(*refs))

            @pl.when(step >= nt)
            def _():
                refs[-2][...] = jnp.zeros(refs[-2].shape, F32)
                refs[-1][...] = jnp.zeros(refs[-1].shape, F32)

    src = lambda i: jnp.minimum(i, nt - 1)
    in_specs = [pl.BlockSpec((tm, a.shape[1]), lambda i: (src(i), 0)) for a in row_inputs]
    in_specs.append(pl.BlockSpec((tm, D_MODEL), lambda i: (src(i) + xo, 0)))
    in_specs += [pl.BlockSpec(a.shape, lambda i, nd=a.ndim: (0,) * nd) for a in fixed_inputs]
    args = list(row_inputs) + [x_arr] + list(fixed_inputs)
    aliases = {}
    if prev is not None:
        in_specs += [pl.BlockSpec(memory_space=pl.ANY)] * 2
        aliases = {len(args): 0, len(args) + 1: 1}
        args += list(prev)
    return pl.pallas_call(
        kernel_fn,
        grid=(nt + int(tail),),
        in_specs=in_specs,
        out_specs=[pl.BlockSpec((tm, D_MODEL), lambda i: (i + oo, 0)),
                   pl.BlockSpec((tm * ROW_TILES, LANES), lambda i: (i + oo, 0))],
        out_shape=[jax.ShapeDtypeStruct((out_rows, D_MODEL), F32),
                   jax.ShapeDtypeStruct((out_rows * ROW_TILES, LANES), F32)],
        input_output_aliases=aliases,
        compiler_params=_cp("parallel"),
        name=name,
    )(*args)


def _inproj_odd_kernel(x_ref, w_ref, wdt_ref, z_ref, xbc_ref, dt_ref):
    xb = x_ref[...].astype(BF16)
    z_ref[...] = jnp.dot(xb, w_ref[:, 0:SSD_D_INNER], preferred_element_type=F32)
    xbc_ref[...] = jnp.dot(xb, w_ref[:, SSD_D_INNER:SSD_MAIN], preferred_element_type=F32)
    dt_ref[...] = jnp.dot(xb, wdt_ref[...], preferred_element_type=F32)


def _inproj_odd(x_all, row0, n, w, wdt):
    tm = _row_tile(n)
    off = row0 // tm
    widths = (SSD_D_INNER, SSD_CONV_DIM, LANES)
    return pl.pallas_call(
        _inproj_odd_kernel,
        grid=(n // tm,),
        in_specs=[pl.BlockSpec((tm, D_MODEL), lambda i: (i + off, 0)),
                  pl.BlockSpec((D_MODEL, SSD_MAIN), lambda i: (0, 0)),
                  pl.BlockSpec((D_MODEL, LANES), lambda i: (0, 0))],
        out_specs=[pl.BlockSpec((tm, c), lambda i: (i, 0)) for c in widths],
        out_shape=[jax.ShapeDtypeStruct((n, c), F32) for c in widths],
        compiler_params=_cp("parallel"),
        name="inproj_odd",
    )(x_all, w, wdt)


def _ssd_kernel(xbc_ref, dt_ref, cbuf_ref, h0_ref, cw_ref, cb_ref, dtb_ref, a_ref, dsk_ref,
                y_ref, cnew_ref, hT_ref, xx, dq, S, *, tt):
    c = pl.program_id(1)
    q = SSD_Q
    hist = CONV_W - 1
    base = 8

    @pl.when(c == 0)
    def _():
        if tt < q:
            xx[...] = jnp.zeros(xx.shape, F32)
            dq[...] = jnp.zeros(dq.shape, F32)
        xx[base - hist:base, :] = cbuf_ref[...]
        S[...] = h0_ref[...]

    @pl.when(c > 0)
    def _():
        xx[base - hist:base, :] = xx[base + tt - hist:base + tt, :]

    xx[base:base + tt, :] = xbc_ref[...]
    dq[0:tt, :] = dt_ref[...]

    @pl.when(c == pl.num_programs(1) - 1)
    def _():
        cnew_ref[...] = xx[base + tt - hist:base + tt, :]

    def conv_cols(lo, hi):
        acc = cb_ref[:, lo:hi]
        for k in range(CONV_W):
            acc = acc + cw_ref[k:k + 1, lo:hi] * _bf16_round(xx[base - hist + k:base - hist + k + q, lo:hi])
        return _silu(acc)

    row = lax.broadcasted_iota(I32, (q, q), 0)
    col = lax.broadcasted_iota(I32, (q, q), 1)
    causal = row >= col
    dt = _softplus(dq[...] + dtb_ref[...])
    if tt < q:
        dt = jnp.where(lax.broadcasted_iota(I32, (q, LANES), 0) < tt, dt, 0.0)
    da = dt * a_ref[...]
    cum = jnp.dot(causal.astype(F32), da, precision=lax.Precision.HIGHEST, preferred_element_type=F32)
    cum_t = cum.T
    dt_t = dt.T
    cum_last = cum[q - 1:q, :]
    ecum = jnp.exp(cum)
    dtde = dt * jnp.exp(cum_last - cum)
    cd = jnp.exp(cum_last)
    lo = lax.broadcasted_iota(I32, (q, LANES), 1) < SSD_P
    lo1 = lo[0:1, :]
    nt = (((1,), (1,)), ((), ()))
    pairs_per_group = SSD_HEADS // SSD_GROUPS // 2

    for g in range(SSD_GROUPS):
        bm = conv_cols(SSD_D_INNER + g * SSD_N, SSD_D_INNER + (g + 1) * SSD_N)
        cm = conv_cols(SSD_D_INNER + SSD_GN + g * SSD_N, SSD_D_INNER + SSD_GN + (g + 1) * SSD_N)
        bmb, cmb = bm.astype(BF16), cm.astype(BF16)
        cb = lax.dot_general(cmb, bmb, nt, preferred_element_type=F32)
        bm_t = bm.T.astype(BF16)
        for pp in range(pairs_per_group):
            p = g * pairs_per_group + pp
            h0, h1 = 2 * p, 2 * p + 1
            cs = slice(p * LANES, (p + 1) * LANES)
            xs = conv_cols(p * LANES, (p + 1) * LANES)
            xsb = xs.astype(BF16)
            yd = []
            for h in (h0, h1):
                seg = cum[:, h:h + 1] - cum_t[h:h + 1, :]
                m = cb * jnp.where(causal, jnp.exp(jnp.where(causal, seg, 0.0)), 0.0) * dt_t[h:h + 1, :]
                yd.append(jnp.dot(m.astype(BF16), xsb, preferred_element_type=F32))
            s_old = S[:, cs]
            yo = jnp.dot(cmb, s_old.astype(BF16), preferred_element_type=F32)
            yo = yo * jnp.where(lo, ecum[:, h0:h0 + 1], ecum[:, h1:h1 + 1])
            y = jnp.where(lo, yd[0], yd[1]) + yo + dsk_ref[:, cs] * xs
            y_ref[:, cs] = y[0:tt, :]
            wts = jnp.where(lo, dtde[:, h0:h0 + 1], dtde[:, h1:h1 + 1])
            upd = jnp.dot(bm_t, (xs * wts).astype(BF16), preferred_element_type=F32)
            S[:, cs] = s_old * jnp.where(lo1, cd[:, h0:h0 + 1], cd[:, h1:h1 + 1]) + upd

    @pl.when(c == pl.num_programs(1) - 1)
    def _():
        hT_ref[...] = S[...]


def _ssd(xbc, dt, cbuf, h0t, cw, cb, dtb, a, dsk):
    bsz, t, _ = xbc.shape
    tt = min(t, SSD_Q)
    row = lambda b, c: (b, c, 0)
    per_b = lambda b, c: (b, 0, 0)
    fixed = lambda b, c: (0, 0)
    return pl.pallas_call(
        functools.partial(_ssd_kernel, tt=tt),
        grid=(bsz, t // tt),
        in_specs=[pl.BlockSpec((None, tt, SSD_CONV_DIM), row), pl.BlockSpec((None, tt, LANES), row),
                  pl.BlockSpec((None, CONV_W - 1, SSD_CONV_DIM), per_b),
                  pl.BlockSpec((None, SSD_N, SSD_D_INNER), per_b),
                  pl.BlockSpec((CONV_W, SSD_CONV_DIM), fixed), pl.BlockSpec((1, SSD_CONV_DIM), fixed),
                  pl.BlockSpec((1, LANES), fixed), pl.BlockSpec((1, LANES), fixed),
                  pl.BlockSpec((1, SSD_D_INNER), fixed)],
        out_specs=[pl.BlockSpec((None, tt, SSD_D_INNER), row),
                   pl.BlockSpec((None, CONV_W - 1, SSD_CONV_DIM), per_b),
                   pl.BlockSpec((None, SSD_N, SSD_D_INNER), per_b)],
        out_shape=[jax.ShapeDtypeStruct((bsz, t, SSD_D_INNER), F32),
                   jax.ShapeDtypeStruct((bsz, CONV_W - 1, SSD_CONV_DIM), F32),
                   jax.ShapeDtypeStruct((bsz, SSD_N, SSD_D_INNER), F32)],
        scratch_shapes=[pltpu.VMEM((8 + SSD_Q, SSD_CONV_DIM), F32), pltpu.VMEM((SSD_Q, LANES), F32),
                        pltpu.VMEM((SSD_N, SSD_D_INNER), F32)],
        compiler_params=_cp("parallel", "arbitrary"),
        name="ssd",
    )(xbc, dt, cbuf, h0t, cw, cb, dtb, a, dsk)


def _outproj_odd_kernel(*refs):
    y_ref, z_ref, x_ref, nw_ref, w_ref, g_ref, b_ref = refs[:7]
    o_ref, ot_ref = refs[-2:]
    gw = SSD_D_INNER // SSD_GROUPS
    acc = None
    for g in range(SSD_GROUPS):
        cs = slice(g * gw, (g + 1) * gw)
        v = y_ref[:, cs] * _silu(z_ref[:, cs])
        v = v * lax.rsqrt(jnp.mean(v * v, -1, keepdims=True) + RMS_EPS) * nw_ref[:, cs]
        part = jnp.dot(v.astype(BF16), w_ref[cs, :], preferred_element_type=F32)
        acc = part if acc is None else acc + part
    out = _ln_rows(ALPHA * x_ref[...] + acc, g_ref[...], b_ref[...])
    o_ref[...] = out
    _store_row_tiles(ot_ref, out)


def _router_kernel(x_ref, w_ref, b_ref, idx_ref, gate_ref, rank_ref, cnt_ref, carry, *, tm):
    i = pl.program_id(0)

    @pl.when(i == 0)
    def _():
        carry[...] = jnp.zeros(carry.shape, F32)

    logits = jnp.dot(x_ref[...].astype(BF16), w_ref[...].astype(BF16), preferred_element_type=F32) + b_ref[...]
    lane = lax.broadcasted_iota(I32, (tm, N_EXPERTS), 1)
    work = logits
    vals, hots = [], []
    for _ in range(TOP_K):
        m = jnp.max(work, axis=1, keepdims=True)
        ik = jnp.min(jnp.where(work == m, lane, N_EXPERTS), axis=1, keepdims=True)
        hot = lane == ik
        vals.append(m)
        hots.append(hot)
        work = jnp.where(hot, NEG_INF, work)
    es = [jnp.exp(v - vals[0]) for v in vals]
    den = es[0] + es[1] + es[2] + es[3]
    hot_all = (hots[0] | hots[1] | hots[2] | hots[3])
    r_i = lax.broadcasted_iota(I32, (tm, tm), 0)
    c_i = lax.broadcasted_iota(I32, (tm, tm), 1)
    before = (r_i > c_i).astype(BF16)
    pre = jnp.dot(before, hot_all.astype(BF16), preferred_element_type=F32) + carry[...]
    lane4 = lax.broadcasted_iota(I32, (tm, TOP_K), 1)
    idx = jnp.zeros((tm, TOP_K), I32)
    gate = jnp.zeros((tm, TOP_K), F32)
    rank = jnp.zeros((tm, TOP_K), I32)
    for k in range(TOP_K):
        ik = jnp.sum(jnp.where(hots[k], lane, 0), axis=1, keepdims=True)
        rk = jnp.sum(jnp.where(hots[k], pre, 0.0), axis=1, keepdims=True).astype(I32)
        idx = jnp.where(lane4 == k, ik, idx)
        rank = jnp.where(lane4 == k, rk, rank)
        gate = jnp.where(lane4 == k, es[k] / den, gate)
    idx_ref[...] = idx
    gate_ref[...] = gate
    rank_ref[...] = rank
    carry[...] = carry[...] + jnp.sum(hot_all.astype(F32), axis=0, keepdims=True)
    cnt_ref[...] = carry[...]


def _router(x_all, w, b):
    n = x_all.shape[0]
    tm = 384 if n % 384 == 0 else 128
    out4 = lambda dt: jax.ShapeDtypeStruct((n, TOP_K), dt)
    return pl.pallas_call(
        functools.partial(_router_kernel, tm=tm),
        grid=(n // tm,),
        in_specs=[pl.BlockSpec((tm, D_MODEL), lambda i: (i, 0)),
                  pl.BlockSpec((D_MODEL, N_EXPERTS), lambda i: (0, 0)),
                  pl.BlockSpec((1, N_EXPERTS), lambda i: (0, 0))],
        out_specs=[pl.BlockSpec((tm, TOP_K), lambda i: (i, 0))] * 3 + [pl.BlockSpec((1, N_EXPERTS), lambda i: (0, 0))],
        out_shape=[out4(I32), out4(F32), out4(I32), jax.ShapeDtypeStruct((1, N_EXPERTS), F32)],
        scratch_shapes=[pltpu.VMEM((1, N_EXPERTS), F32)],
        compiler_params=_cp("arbitrary"),
        name="router",
    )(x_all, w, b)


def _tile_copy(src_hbm, row8, dst, dst_row8, sem):
    return pltpu.make_async_copy(src_hbm.at[pl.ds(pl.multiple_of(row8, ROW_TILES), ROW_TILES), :],
                                 dst.at[pl.ds(pl.multiple_of(dst_row8, ROW_TILES), ROW_TILES), :], sem)


def _dispatch_kernel(pos_ref, xt_ref, xs_init_hbm, xs_hbm, sem, *, tm):
    del xs_init_hbm
    i = pl.program_id(0)

    def issue(t, _):
        src = xt_ref.at[pl.ds(pl.multiple_of(t * ROW_TILES, ROW_TILES), ROW_TILES), :]
        for k in range(TOP_K):
            slot8 = pl.multiple_of(pos_ref[(i * tm + t) * TOP_K + k], ROW_TILES)
            pltpu.make_async_copy(src, xs_hbm.at[pl.ds(slot8, ROW_TILES), :], sem).start(priority=k % 2)
        return 0

    lax.fori_loop(0, tm, issue, 0, unroll=4)
    for _ in range(TOP_K):
        pltpu.make_async_copy(xt_ref, xs_hbm.at[pl.ds(0, tm * ROW_TILES), :], sem).wait()


def _dispatch(pos8_flat, xt_all, n_slots, slot_init):
    if slot_init is None:
        slot_init = jnp.zeros((n_slots * ROW_TILES, LANES), F32)
    n = xt_all.shape[0] // ROW_TILES
    tm = 128
    grid_spec = pltpu.PrefetchScalarGridSpec(
        num_scalar_prefetch=1,
        grid=(n // tm,),
        in_specs=[pl.BlockSpec((tm * ROW_TILES, LANES), lambda i, pos: (i, 0)),
                  pl.BlockSpec(memory_space=pl.ANY)],
        out_specs=pl.BlockSpec(memory_space=pl.ANY),
        scratch_shapes=[pltpu.SemaphoreType.DMA],
    )
    return pl.pallas_call(
        functools.partial(_dispatch_kernel, tm=tm),
        grid_spec=grid_spec,
        out_shape=jax.ShapeDtypeStruct((n_slots * ROW_TILES, LANES), F32),
        input_output_aliases={2: 0},
        compiler_params=_cp("arbitrary"),
        name="moe_dispatch",
    )(pos8_flat, xt_all, slot_init)


def _ffn_kernel(te_ref, nv_ref, xs_ref, w1_ref, b1_ref, w2_ref, b2_ref, ot_ref, w1b, w2i, w2b, *, rows):
    i = pl.program_id(0)
    nvalid = nv_ref[0]
    half = D_FF // 2

    @pl.when((i < nvalid) & ((i == 0) | (te_ref[i] != te_ref[jnp.maximum(i - 1, 0)])))
    def _():
        w1b[...] = w1_ref[...].astype(BF16)
        for c in range(ROW_TILES):
            cs = slice(c * LANES, (c + 1) * LANES)
            w2i[c, pl.ds(0, half, stride=2), :] = w2_ref[0:half, cs]
            w2i[c, pl.ds(1, half, stride=2), :] = w2_ref[half:D_FF, cs]
            w2b[:, cs] = w2i[c].astype(BF16)

    @pl.when(i < nvalid)
    def _():
        xb = _load_row_tiles(xs_ref, rows).astype(BF16)
        h = jnp.dot(xb, w1b[...], preferred_element_type=F32) + b1_ref[...]
        even = (lax.broadcasted_iota(I32, (rows, LANES), 1) % 2) == 0
        parts = []
        for c in range(D_FF // LANES):
            ha = h[:, c * LANES:(c + 1) * LANES]
            hb = h[:, D_FF + c * LANES:D_FF + (c + 1) * LANES]
            glu = jnp.minimum(jnp.where(even, ha, pltpu.roll(hb, 1, 1)), SWIGLU_LIMIT)
            lin = jnp.clip(jnp.where(even, pltpu.roll(ha, LANES - 1, 1), hb), -SWIGLU_LIMIT, SWIGLU_LIMIT)
            parts.append((glu * _sigmoid(SWIGLU_ALPHA * glu) * (lin + 1.0)).astype(BF16))
        act = jnp.concatenate(parts, axis=1)
        _store_row_tiles(ot_ref, jnp.dot(act, w2b[...], preferred_element_type=F32) + b2_ref[...])

    @pl.when(i >= nvalid)
    def _():
        ot_ref[...] = jnp.zeros(ot_ref.shape, F32)


def _ffn(xs_t, tile_expert, nvalid, layer, w1, b1, w2, b2, ntiles):
    rows = MOE_ROWS
    wspec = lambda shape: pl.BlockSpec((None, None) + shape, lambda i, te, nv: (layer, te[i], 0, 0))
    grid_spec = pltpu.PrefetchScalarGridSpec(
        num_scalar_prefetch=2,
        grid=(ntiles,),
        in_specs=[pl.BlockSpec((rows * ROW_TILES, LANES), lambda i, te, nv: (i, 0)),
                  wspec((D_MODEL, 2 * D_FF)), wspec((1, 2 * D_FF)),
                  wspec((D_FF, D_MODEL)), wspec((1, D_MODEL))],
        out_specs=pl.BlockSpec((rows * ROW_TILES, LANES), lambda i, te, nv: (i, 0)),
        scratch_shapes=[pltpu.VMEM((D_MODEL, 2 * D_FF), BF16), pltpu.VMEM((ROW_TILES, D_FF, LANES), F32),
                        pltpu.VMEM((D_FF, D_MODEL), BF16)],
    )
    return pl.pallas_call(
        functools.partial(_ffn_kernel, rows=rows),
        grid_spec=grid_spec,
        out_shape=jax.ShapeDtypeStruct((ntiles * rows * ROW_TILES, LANES), F32),
        compiler_params=_cp("arbitrary"),
        name="moe_ffn",
    )(tile_expert, nvalid, xs_t, w1, b1, w2, b2)


def _combine_kernel(pos_ref, ot_hbm, gate_ref, x_ref, g_ref, b_ref, out_ref, buf, sem, *, tm):
    i = pl.program_id(0)
    n = pl.num_programs(0)
    slot = i % 2

    def gather(tile, s):
        def issue(t, _):
            for k in range(TOP_K):
                _tile_copy(ot_hbm, pos_ref[(tile * tm + t) * TOP_K + k], buf.at[s, k], t * ROW_TILES,
                           sem.at[s]).start(priority=k % 2)
            return 0
        lax.fori_loop(0, tm, issue, 0, unroll=4)

    @pl.when(i == 0)
    def _():
        gather(0, 0)

    @pl.when(i + 1 < n)
    def _():
        gather(i + 1, 1 - slot)

    for k in range(TOP_K):
        pltpu.make_async_copy(ot_hbm.at[pl.ds(0, tm * ROW_TILES), :], buf.at[slot, k], sem.at[slot]).wait()
    f = None
    for k in range(TOP_K):
        part = gate_ref[:, k:k + 1] * _load_row_tiles(buf.at[slot, k], tm)
        f = part if f is None else f + part
    out_ref[...] = _ln_rows(ALPHA * x_ref[...] + f, g_ref[...], b_ref[...])


def _combine(pos_flat, ot_sorted, gate, x_all, g, b):
    n = x_all.shape[0]
    tm = 128
    grid_spec = pltpu.PrefetchScalarGridSpec(
        num_scalar_prefetch=1,
        grid=(n // tm,),
        in_specs=[pl.BlockSpec(memory_space=pl.ANY),
                  pl.BlockSpec((tm, TOP_K), lambda i, pos: (i, 0)),
                  pl.BlockSpec((tm, D_MODEL), lambda i, pos: (i, 0)),
                  pl.BlockSpec((1, D_MODEL), lambda i, pos: (0, 0)),
                  pl.BlockSpec((1, D_MODEL), lambda i, pos: (0, 0))],
        out_specs=pl.BlockSpec((tm, D_MODEL), lambda i, pos: (i, 0)),
        scratch_shapes=[pltpu.VMEM((2, TOP_K, tm * ROW_TILES, LANES), F32), pltpu.SemaphoreType.DMA((2,))],
    )
    return pl.pallas_call(
        functools.partial(_combine_kernel, tm=tm),
        grid_spec=grid_spec,
        out_shape=jax.ShapeDtypeStruct((n, D_MODEL), F32),
        compiler_params=_cp("arbitrary"),
        name="moe_combine",
    )(pos_flat, ot_sorted, gate, x_all, g, b)


def _moe(x_all, xt_all, layer, w_r, b_r, w1, b1, w2, b2, g, b, slot_init=None):
    n = x_all.shape[0]
    rows = MOE_ROWS
    ntiles = -(-(n * TOP_K + N_EXPERTS * (rows - 1)) // rows)
    idx, gate, rank, cnt = _router(x_all, w_r, b_r.reshape(1, N_EXPERTS))
    counts = cnt[0].astype(I32)
    padded = ((counts + rows - 1) // rows) * rows
    ends = jnp.cumsum(padded)
    starts = ends - padded
    experts = jnp.arange(N_EXPERTS, dtype=I32)
    pos = rank + jnp.sum(jnp.where(idx[:, :, None] == experts, starts, 0), axis=-1)
    tile_starts = jnp.arange(ntiles, dtype=I32) * rows
    tile_expert = jnp.minimum(jnp.sum((tile_starts[:, None] >= ends[None, :]).astype(I32), axis=1), N_EXPERTS - 1)
    nvalid = (ends[-1:] // rows).astype(I32)
    pos8 = pos.reshape(-1) * ROW_TILES
    xs_t = _dispatch(pos8, xt_all, ntiles * rows, slot_init)
    ot_sorted = _ffn(xs_t, tile_expert, nvalid, layer, w1, b1[:, :, None, :], w2, b2[:, :, None, :], ntiles)
    return _combine(pos8, ot_sorted, gate, x_all, g.reshape(1, -1), b.reshape(1, -1)), ot_sorted


def _block_diag(w):
    nb, bw, _ = w.shape
    eye = jnp.eye(nb, dtype=w.dtype)
    return (eye[:, None, :, None] * w[:, :, None, :]).reshape(nb * bw, nb * bw)


def _mix_even(x_arr, out_row0, n_total, bsz, t, rg_conv, rg_h, cache, p, prev):
    n = bsz * t
    xa, ga, q, kv = _inproj_even(x_arr, 0, n, p["w_in"])
    ya, conv_new, h_last = _rglru(
        xa.reshape(bsz, t, RG_WIDTH), ga.reshape(bsz, t, RG_WIDTH), rg_conv, rg_h.reshape(bsz, 1, RG_WIDTH),
        p["rg_conv_w"], p["rg_conv_b"], p["rg_wg"], p["rg_bg"], p["rg_c"])
    q3 = q.reshape(bsz, t, B_WIDTH)
    kv3 = kv.reshape(bsz, t, 2 * B_WIDTH)
    if cache is None:
        yb = _attn_prompt(q3, kv3, p["rel_bias"])
    else:
        yb = _attn_sample(q3, cache, kv3, p["rel_bias"])
    x1 = _token_call(_outproj_even_kernel, "outproj_even", n, 0, n_total, out_row0,
                     [ya.reshape(n, RG_WIDTH), yb.reshape(n, B_WIDTH)], x_arr,
                     [p["w_out"], p["ln_g0"], p["ln_b0"]], prev)
    return x1, conv_new, h_last.reshape(bsz, RG_WIDTH), kv3


def _mix_odd(x_all, row0, bsz, t, ssd_conv, ssd_h, p, prev):
    n = bsz * t
    z, xbc, dt = _inproj_odd(x_all, row0, n, p["ssd_w_main"], p["ssd_w_dt"])
    h0t = jnp.transpose(ssd_h.astype(F32), (0, 3, 1, 2)).reshape(bsz, SSD_N, SSD_D_INNER)
    y, conv_new, h_t = _ssd(xbc.reshape(bsz, t, SSD_CONV_DIM), dt.reshape(bsz, t, LANES), ssd_conv, h0t,
                            p["ssd_conv_w"], p["ssd_conv_b"], p["ssd_dtb"], p["ssd_a"], p["ssd_dsk"])
    h_last = jnp.transpose(h_t.reshape(bsz, SSD_N, SSD_HEADS, SSD_P), (0, 2, 3, 1))
    x1 = _token_call(_outproj_odd_kernel, "outproj_odd", n, row0, x_all.shape[0], row0,
                     [y.reshape(n, SSD_D_INNER), z], x_all,
                     [p["ssd_norm_w"], p["ssd_w_out"], p["ln_g2"], p["ln_b2"]], prev)
    return x1, conv_new, h_last


def _mixer_params(rel_bias, w_in_mix, rg_conv_w, rg_conv_b, rg_w_a, rg_b_a, rg_w_i, rg_b_i, rg_lambda, w_out_mix,
                  ssd_w_in, ssd_conv_w, ssd_conv_b, ssd_dt_bias, ssd_a_log, ssd_d, ssd_norm_w, ssd_w_out, ln_g, ln_b):
    row2 = lambda v: v.reshape(1, -1).astype(F32)
    pad_heads = lambda v: jnp.pad(v.astype(F32), (0, LANES - SSD_HEADS)).reshape(1, LANES)
    return dict(
        rel_bias=rel_bias,
        w_in=w_in_mix[0].astype(BF16),
        rg_conv_w=rg_conv_w[0].astype(F32), rg_conv_b=row2(rg_conv_b[0]),
        rg_wg=jnp.concatenate([_block_diag(rg_w_a[0]), _block_diag(rg_w_i[0])], axis=1).astype(BF16),
        rg_bg=jnp.concatenate([rg_b_a[0].reshape(1, -1), rg_b_i[0].reshape(1, -1)], axis=1).astype(F32),
        rg_c=row2(-RG_C * jax.nn.softplus(-rg_lambda[0].astype(F32))),
        w_out=w_out_mix[0].astype(BF16),
        ln_g0=row2(ln_g[0, 0]), ln_b0=row2(ln_b[0, 0]),
        ssd_w_main=ssd_w_in[0][:, :SSD_MAIN].astype(BF16),
        ssd_w_dt=jnp.pad(ssd_w_in[0][:, SSD_MAIN:], ((0, 0), (0, LANES - SSD_HEADS))).astype(BF16),
        ssd_conv_w=ssd_conv_w[0].astype(F32), ssd_conv_b=row2(ssd_conv_b[0]),
        ssd_dtb=pad_heads(ssd_dt_bias[0]), ssd_a=pad_heads(-jnp.exp(ssd_a_log[0].astype(F32))),
        ssd_dsk=row2(jnp.repeat(ssd_d[0].astype(F32), SSD_P)),
        ssd_norm_w=row2(ssd_norm_w[0]), ssd_w_out=ssd_w_out[0].astype(BF16),
        ln_g2=row2(ln_g[1, 0]), ln_b2=row2(ln_b[1, 0]),
    )


def kernel(x_prompt, x_sample, state_rglru_conv, state_rglru_h, cache_swa_kv, state_ssd_conv, state_ssd_h,
           rel_bias, w_in_mix, rg_conv_w, rg_conv_b, rg_w_a, rg_b_a, rg_w_i, rg_b_i, rg_lambda, w_out_mix,
           ssd_w_in, ssd_conv_w, ssd_conv_b, ssd_dt_bias, ssd_a_log, ssd_d, ssd_norm_w, ssd_w_out,
           ln_g, ln_b, router_w, router_b, exp_w1, exp_b1, exp_w2, exp_b2):
    bp, tp, _ = x_prompt.shape
    bs, ts, _ = x_sample.shape
    n_p, n_s = bp * tp, bs * ts
    n_total = n_p + n_s
    p = _mixer_params(rel_bias, w_in_mix, rg_conv_w, rg_conv_b, rg_w_a, rg_b_a, rg_w_i, rg_b_i, rg_lambda, w_out_mix,
                      ssd_w_in, ssd_conv_w, ssd_conv_b, ssd_dt_bias, ssd_a_log, ssd_d, ssd_norm_w, ssd_w_out, ln_g, ln_b)

    xp = x_prompt.reshape(n_p, D_MODEL)
    xs = x_sample.reshape(n_s, D_MODEL)
    zeros = lambda *s: jnp.zeros(s, F32)
    x1, p_rg_conv, p_rg_h, p_kv = _mix_even(xp, 0, n_total, bp, tp, zeros(bp, CONV_W - 1, RG_WIDTH),
                                            zeros(bp, RG_WIDTH), None, p, None)
    cache = cache_swa_kv[0].reshape(bs, cache_swa_kv.shape[2], 2 * B_WIDTH)
    x1, s_rg_conv, s_rg_h, s_kv = _mix_even(xs, n_p, n_total, bs, ts, state_rglru_conv[0], state_rglru_h[0],
                                            cache, p, x1)
    x2, spent_slots = _moe(x1[0], x1[1], 0, router_w[0].astype(F32), router_b[0].astype(F32), exp_w1, exp_b1, exp_w2,
                           exp_b2, ln_g[0, 1], ln_b[0, 1])

    x3, p_ssd_conv, p_ssd_h = _mix_odd(x2, 0, bp, tp, zeros(bp, CONV_W - 1, SSD_CONV_DIM),
                                       zeros(bp, SSD_HEADS, SSD_P, SSD_N), p, None)
    x3, s_ssd_conv, s_ssd_h = _mix_odd(x2, n_p, bs, ts, state_ssd_conv[0], state_ssd_h[0], p, x3)
    x4, _ = _moe(x3[0], x3[1], 1, router_w[1].astype(F32), router_b[1].astype(F32), exp_w1, exp_b1, exp_w2, exp_b2,
                 ln_g[1, 1], ln_b[1, 1], slot_init=spent_slots)

    kv_shape = lambda b, t: (1, b, t, 2, B_HEADS, B_HD)
    return (x4[:n_p].reshape(bp, tp, D_MODEL), x4[n_p:].reshape(bs, ts, D_MODEL),
            p_rg_conv[None], p_rg_h[None], p_kv[:, -min(MAX_WINDOW, tp):].reshape(kv_shape(bp, min(MAX_WINDOW, tp))),
            p_ssd_conv[None], p_ssd_h[None],
            s_rg_conv[None], s_rg_h[None], s_kv.reshape(kv_shape(bs, ts)),
            s_ssd_conv[None], s_ssd_h[None])
```

```python
import functools
import math

import numpy as np
import jax
import jax.numpy as jnp
from jax import lax
from jax.experimental import pallas as pl
from jax.experimental.pallas import tpu as pltpu

F32 = jnp.float32
BF16 = jnp.bfloat16
I32 = jnp.int32

D_MODEL = 1024
DEPTH = 2
ALPHA = (2 * DEPTH) ** 0.25
LN_EPS = 1e-5
RMS_EPS = 1e-5
CONV_W = 4
RG_BLOCKS = 12
RG_BW = 64
RG_WIDTH = RG_BLOCKS * RG_BW
RG_C = 8.0
B_HEADS = 12
B_HD = 64
B_WIDTH = B_HEADS * B_HD
PATTERNS = ((128, 1), (512, 4), (2048, 16))
MAX_WINDOW = 2048
N_BUCKETS = 32
MAX_DISTANCE = MAX_WINDOW
EVEN_IN = 2 * RG_WIDTH + 3 * B_WIDTH
SSD_D_INNER = 2 * D_MODEL
SSD_P = 64
SSD_HEADS = SSD_D_INNER // SSD_P
SSD_N = 128
SSD_GROUPS = 4
SSD_GN = SSD_GROUPS * SSD_N
SSD_CONV_DIM = SSD_D_INNER + 2 * SSD_GN
SSD_MAIN = SSD_D_INNER + SSD_CONV_DIM
N_EXPERTS = 32
TOP_K = 4
D_FF = D_MODEL
SWIGLU_LIMIT = 7.0
SWIGLU_ALPHA = 1.702

LANES = 128
VMEM_LIMIT = 56 * 1024 * 1024

SSD_Q = 128
ATT_BLK = 256
ATT_CHUNK = 64
MOE_ROWS = 256
NEG_INF = float("-inf")


ROW_TILES = D_MODEL // LANES


def _cp(*sem):
    return pltpu.CompilerParams(dimension_semantics=sem, vmem_limit_bytes=VMEM_LIMIT)


def _store_row_tiles(ref, val):
    rows = val.shape[0]
    for s in range(ROW_TILES):
        ref[pl.ds(s, rows, stride=ROW_TILES), :] = val[:, s * LANES:(s + 1) * LANES]


def _load_row_tiles(ref, rows):
    return jnp.concatenate([ref[pl.ds(s, rows, stride=ROW_TILES), :] for s in range(ROW_TILES)], axis=1)


def _ln_rows(v, g, b):
    mu = jnp.mean(v, -1, keepdims=True)
    xc = v - mu
    var = jnp.mean(xc * xc, -1, keepdims=True)
    return xc * lax.rsqrt(var + LN_EPS) * g + b


def _bf16_round(x):
    return x.astype(BF16).astype(F32)


def _sigmoid(x):
    return 1.0 / (1.0 + jnp.exp(-x))


def _silu(x):
    return x * _sigmoid(x)


def _softplus(x):
    return jnp.maximum(x, 0.0) + jnp.log1p(jnp.exp(-jnp.abs(x)))


def _gelu_tanh(x):
    return 0.5 * x * (1.0 + jnp.tanh(math.sqrt(2.0 / math.pi) * (x + 0.044715 * (x * x * x))))


def _row_tile(n):
    return min(n, 256)


def _inproj_even_kernel(x_ref, w_ref, xa_ref, ga_ref, q_ref, kv_ref):
    xb = x_ref[...].astype(BF16)
    c0, c1, c2 = RG_WIDTH, 2 * RG_WIDTH, 2 * RG_WIDTH + B_WIDTH
    xa_ref[...] = jnp.dot(xb, w_ref[:, 0:c0], preferred_element_type=F32)
    ga_ref[...] = jnp.dot(xb, w_ref[:, c0:c1], preferred_element_type=F32)
    q_ref[...] = jnp.dot(xb, w_ref[:, c1:c2], preferred_element_type=F32)
    kv_ref[...] = jnp.dot(xb, w_ref[:, c2:EVEN_IN], preferred_element_type=F32)


def _inproj_even(x_all, row0, n, w):
    tm = _row_tile(n)
    off = row0 // tm
    widths = (RG_WIDTH, RG_WIDTH, B_WIDTH, 2 * B_WIDTH)
    return pl.pallas_call(
        _inproj_even_kernel,
        grid=(n // tm,),
        in_specs=[pl.BlockSpec((tm, D_MODEL), lambda i: (i + off, 0)),
                  pl.BlockSpec((D_MODEL, EVEN_IN), lambda i: (0, 0))],
        out_specs=[pl.BlockSpec((tm, c), lambda i: (i, 0)) for c in widths],
        out_shape=[jax.ShapeDtypeStruct((n, c), F32) for c in widths],
        compiler_params=_cp("parallel"),
        name="inproj_even",
    )(x_all, w)


def _rglru_kernel(xa_ref, ga_ref, cbuf_ref, h0_ref, cw_ref, cb_ref, wg_ref, bg_ref, c_ref,
                  ya_ref, cnew_ref, hlast_ref, xx, a_s, u_s, hc, *, tt):
    j = pl.program_id(1)
    hist = CONV_W - 1
    base = 8

    @pl.when(j == 0)
    def _():
        xx[base - hist:base, :] = cbuf_ref[...]
        hc[...] = h0_ref[...]

    @pl.when(j > 0)
    def _():
        xx[base - hist:base, :] = xx[base + tt - hist:base + tt, :]

    xx[base:base + tt, :] = xa_ref[...]
    conv = cb_ref[...]
    for k in range(CONV_W):
        conv = conv + cw_ref[k:k + 1, :] * _bf16_round(xx[base - hist + k:base - hist + k + tt, :])
    gates = jnp.dot(conv.astype(BF16), wg_ref[...], preferred_element_type=F32) + bg_ref[...]
    r = _sigmoid(gates[:, :RG_WIDTH])
    ig = _sigmoid(gates[:, RG_WIDTH:])
    log_a = c_ref[...] * r
    a = jnp.exp(log_a)
    a_s[...] = a
    u_s[...] = jnp.sqrt(-jnp.tanh(log_a) * (a * a + 1.0)) * (ig * conv)

    def step(t, h):
        h = a_s[pl.ds(t, 1), :] * h + u_s[pl.ds(t, 1), :]
        u_s[pl.ds(t, 1), :] = h
        return h

    h = lax.fori_loop(0, tt, step, hc[...], unroll=min(tt, 8))
    hc[...] = h
    ya_ref[...] = _gelu_tanh(ga_ref[...]) * u_s[...]

    @pl.when(j == pl.num_programs(1) - 1)
    def _():
        hlast_ref[...] = h
        cnew_ref[...] = xx[base + tt - hist:base + tt, :]


def _rglru(xa, ga, cbuf, h0, cw, cb, wg, bg, cvec):
    bsz, t, c = xa.shape
    tt = min(t, 256)
    row = lambda b, j: (b, j, 0)
    per_b = lambda b, j: (b, 0, 0)
    fixed = lambda b, j: (0, 0)
    return pl.pallas_call(
        functools.partial(_rglru_kernel, tt=tt),
        grid=(bsz, t // tt),
        in_specs=[pl.BlockSpec((None, tt, c), row), pl.BlockSpec((None, tt, c), row),
                  pl.BlockSpec((None, CONV_W - 1, c), per_b), pl.BlockSpec((None, 1, c), per_b),
                  pl.BlockSpec((CONV_W, c), fixed), pl.BlockSpec((1, c), fixed),
                  pl.BlockSpec((c, 2 * c), fixed), pl.BlockSpec((1, 2 * c), fixed),
                  pl.BlockSpec((1, c), fixed)],
        out_specs=[pl.BlockSpec((None, tt, c), row), pl.BlockSpec((None, CONV_W - 1, c), per_b),
                   pl.BlockSpec((None, 1, c), per_b)],
        out_shape=[jax.ShapeDtypeStruct((bsz, t, c), F32),
                   jax.ShapeDtypeStruct((bsz, CONV_W - 1, c), F32),
                   jax.ShapeDtypeStruct((bsz, 1, c), F32)],
        scratch_shapes=[pltpu.VMEM((8 + tt, c), F32), pltpu.VMEM((tt, c), F32),
                        pltpu.VMEM((tt, c), F32), pltpu.VMEM((1, c), F32)],
        compiler_params=_cp("parallel", "arbitrary"),
        name="rglru",
    )(xa, ga, cbuf, h0, cw, cb, wg, bg, cvec)


def _t5_bucket(dist):
    max_exact = N_BUCKETS // 2
    safe = np.maximum(dist, 1)
    large = max_exact + (np.log(safe / max_exact) / np.log(MAX_DISTANCE / max_exact)
                         * (N_BUCKETS - max_exact)).astype(np.int32)
    return np.where(dist < max_exact, dist, np.minimum(large, N_BUCKETS - 1)).astype(np.int32)


def _distance_bias(rel_bias, max_d):
    d = np.arange(max_d + 1)
    mult = np.zeros(max_d + 1, np.float64)
    for w, dil in PATTERNS:
        mult += ((d % dil == 0) & (d <= w)).astype(np.float64)
    logm = np.where(mult > 0, np.log(np.maximum(mult, 1.0)), -np.inf).astype(np.float32)
    vals = rel_bias.astype(F32)[_t5_bucket(d)].T + jnp.asarray(logm)[None, :]
    return jnp.concatenate([vals, jnp.full((vals.shape[0], 1), NEG_INF, F32)], axis=1)


def _attn_prompt_kernel(q_ref, k_ref, v_ref, rv_ref, o_ref, tab_ref, k_s, v_s, *, blk, nq):
    qi = pl.program_id(2)

    @pl.when((pl.program_id(1) == 0) & (qi == 0))
    def _():
        for h in range(2):
            for d in range(nq):
                win = rv_ref[h:h + 1, (nq - 1 - d) * blk:(nq + 1 - d) * blk]
                rolled = pltpu.roll(jnp.broadcast_to(win, (blk, 2 * blk)), 0, 1, stride=1, stride_axis=0)
                tab_ref[h, d] = rolled[:, blk:]

    lo = lax.broadcasted_iota(I32, (blk, LANES), 1) < B_HD

    @pl.when(qi == 0)
    def _():
        k_s[...] = k_ref[...].astype(BF16)
        v_s[...] = v_ref[...].astype(BF16)

    q2 = q_ref[...] * (B_HD ** -0.5)
    qh = (jnp.where(lo, q2, 0.0).astype(BF16), jnp.where(lo, 0.0, q2).astype(BF16))
    nt = (((1,), (1,)), ((), ()))
    ch = ATT_CHUNK

    def scores(d):
        k = k_s[pl.ds(pl.multiple_of((qi - d) * blk, blk), blk), :]
        return tuple(lax.dot_general(qh[h], k, nt, preferred_element_type=F32) for h in range(2))

    def body(d, carry):
        m0, l0, m1, l1, acc, s0, s1 = carry
        s_next = scores(jnp.minimum(d + 1, qi))
        v = v_s[pl.ds(pl.multiple_of((qi - d) * blk, blk), blk), :]
        new = []
        for h, (m, l, s) in enumerate(((m0, l0, s0), (m1, l1, s1))):
            mns, lns, als, ps = [], [], [], []
            for c in range(blk // ch):
                rs = slice(c * ch, (c + 1) * ch)
                sc = s[rs] + tab_ref[h, d, rs, :]
                mn = jnp.maximum(m[rs], jnp.broadcast_to(jnp.max(sc, axis=1, keepdims=True), (ch, LANES)))
                p = jnp.concatenate([jnp.exp(sc[:, j * LANES:(j + 1) * LANES] - mn) for j in range(blk // LANES)], axis=1)
                al = jnp.exp(m[rs] - mn)
                mns.append(mn)
                als.append(al)
                lns.append(al * l[rs] + jnp.broadcast_to(jnp.sum(p, axis=1, keepdims=True), (ch, LANES)))
                ps.append(p.astype(BF16))
            cat = lambda xs: jnp.concatenate(xs, axis=0)
            new.append((cat(mns), cat(lns), cat(als), jnp.dot(cat(ps), v, preferred_element_type=F32)))
        (m0, l0, a0, pv0), (m1, l1, a1, pv1) = new
        acc = jnp.where(lo, a0, a1) * acc + jnp.where(lo, pv0, pv1)
        return (m0, l0, m1, l1, acc) + s_next

    minf = jnp.full((blk, LANES), NEG_INF, F32)
    zero = jnp.zeros((blk, LANES), F32)
    out = lax.fori_loop(0, qi + 1, body, (minf, zero, minf, zero, zero) + scores(0))
    m0, l0, m1, l1, acc = out[:5]
    o_ref[...] = acc / jnp.where(lo, l0, l1)


def _attn_prompt(q, kv, rel_bias):
    bsz, t, _ = q.shape
    blk = min(ATT_BLK, t)
    nq = t // blk
    npair = B_HEADS // 2
    vals = _distance_bias(rel_bias, t)[:, :t + 1]
    rv = jnp.concatenate([jnp.full((B_HEADS, blk - 1), NEG_INF, F32), vals], axis=1)[:, ::-1]
    rv = rv.reshape(npair, 2, t + blk)
    return pl.pallas_call(
        functools.partial(_attn_prompt_kernel, blk=blk, nq=nq),
        grid=(npair, bsz, nq),
        in_specs=[pl.BlockSpec((None, blk, LANES), lambda p, b, i: (b, i, p)),
                  pl.BlockSpec((None, t, LANES), lambda p, b, i: (b, 0, p)),
                  pl.BlockSpec((None, t, LANES), lambda p, b, i: (b, 0, npair + p)),
                  pl.BlockSpec((None, 2, t + blk), lambda p, b, i: (p, 0, 0))],
        out_specs=pl.BlockSpec((None, blk, LANES), lambda p, b, i: (b, i, p)),
        out_shape=jax.ShapeDtypeStruct((bsz, t, B_WIDTH), F32),
        scratch_shapes=[pltpu.VMEM((2, nq, blk, blk), F32), pltpu.VMEM((t, LANES), BF16),
                        pltpu.VMEM((t, LANES), BF16)],
        compiler_params=_cp("arbitrary", "arbitrary", "arbitrary"),
        name="attn_prompt",
    )(q, kv, kv, rv)


ATT_S_ROWS = 16


def _attn_sample_kernel(qbd_ref, kc_ref, vc_ref, kn_ref, vn_ref, tabc_ref, tabn_ref, mask_ref, o_ref,
                        m_s, l_s, acc_s, *, t_new):
    c = pl.program_id(1)
    rows = t_new * ATT_S_ROWS
    nt = (((1,), (1,)), ((), ()))

    @pl.when(c == 0)
    def _():
        m_s[...] = jnp.full((rows, 1), NEG_INF, F32)
        l_s[...] = jnp.zeros((rows, 1), F32)
        acc_s[...] = jnp.zeros((rows, B_WIDTH), F32)

    qf = qbd_ref[...] * (B_HD ** -0.5)
    qb = qf.astype(BF16)
    s = lax.dot_general(qb, kc_ref[...].astype(BF16), nt, preferred_element_type=F32) + tabc_ref[...]
    m = m_s[...]
    mn = jnp.maximum(m, jnp.max(s, axis=1, keepdims=True))
    last = c == pl.num_programs(1) - 1

    sn = [jnp.sum(qf * kn_ref[j:j + 1, :], axis=1, keepdims=True) + tabn_ref[:, j:j + 1] for j in range(t_new)]
    sn = [jnp.where(last, x, NEG_INF) for x in sn]
    for x in sn:
        mn = jnp.maximum(mn, x)
    safe = jnp.where(mn == NEG_INF, 0.0, mn)
    p = jnp.exp(s - safe)
    al = jnp.exp(m - safe)
    l = al * l_s[...] + jnp.sum(p, axis=1, keepdims=True)
    acc = al * acc_s[...] + jnp.dot(p.astype(BF16), vc_ref[...].astype(BF16), preferred_element_type=F32)
    for j, x in enumerate(sn):
        pj = jnp.exp(x - safe)
        l = l + pj
        acc = acc + pj * vn_ref[j:j + 1, :]
    m_s[...] = mn
    l_s[...] = l
    acc_s[...] = acc

    @pl.when(last)
    def _():
        om = (acc / l) * mask_ref[...]
        for t in range(t_new):
            o_ref[t:t + 1, :] = jnp.sum(om[t * ATT_S_ROWS:(t + 1) * ATT_S_ROWS, :], axis=0, keepdims=True)


def _attn_sample(q, cache, kv_new, rel_bias):
    bsz, t_new, _ = q.shape
    w = cache.shape[1]
    rows = t_new * ATT_S_ROWS
    vals = _distance_bias(rel_bias, MAX_WINDOW)
    vals = jnp.concatenate([vals, jnp.zeros((ATT_S_ROWS - B_HEADS, vals.shape[1]), F32)], axis=0)
    tq = np.repeat(np.arange(t_new), ATT_S_ROWS)
    hq = np.tile(np.arange(ATT_S_ROWS), t_new)
    ext = jnp.concatenate([vals[:, :MAX_WINDOW + 1], jnp.full((ATT_S_ROWS, w + t_new - MAX_WINDOW), NEG_INF, F32)], axis=1)
    ext = ext.at[B_HEADS:, :].set(0.0)
    tabc = jnp.stack([ext[:, t + 1:w + t + 1][:, ::-1] for t in range(t_new)]).reshape(rows, w)
    dist_n = tq[:, None] - np.arange(t_new)[None, :]
    dist_n = np.where(dist_n < 0, MAX_WINDOW + 1, dist_n)
    tabn = vals[jnp.asarray(hq)[:, None], jnp.asarray(dist_n)]
    mask = (hq[:, None] == (np.arange(B_WIDTH)[None, :] // B_HD)).astype(np.float32)
    qbd = (q[:, :, None, :] * jnp.asarray(mask.reshape(t_new, ATT_S_ROWS, B_WIDTH))[None]).reshape(bsz, rows, B_WIDTH)
    kc = 1024
    return pl.pallas_call(
        functools.partial(_attn_sample_kernel, t_new=t_new),
        grid=(bsz, w // kc),
        in_specs=[pl.BlockSpec((None, rows, B_WIDTH), lambda b, c: (b, 0, 0)),
                  pl.BlockSpec((None, kc, B_WIDTH), lambda b, c: (b, c, 0)),
                  pl.BlockSpec((None, kc, B_WIDTH), lambda b, c: (b, c, 1)),
                  pl.BlockSpec((None, t_new, B_WIDTH), lambda b, c: (b, 0, 0)),
                  pl.BlockSpec((None, t_new, B_WIDTH), lambda b, c: (b, 0, 1)),
                  pl.BlockSpec((rows, kc), lambda b, c: (0, c)),
                  pl.BlockSpec((rows, t_new), lambda b, c: (0, 0)),
                  pl.BlockSpec((rows, B_WIDTH), lambda b, c: (0, 0))],
        out_specs=pl.BlockSpec((None, t_new, B_WIDTH), lambda b, c: (b, 0, 0)),
        out_shape=jax.ShapeDtypeStruct((bsz, t_new, B_WIDTH), F32),
        scratch_shapes=[pltpu.VMEM((rows, 1), F32), pltpu.VMEM((rows, 1), F32), pltpu.VMEM((rows, B_WIDTH), F32)],
        compiler_params=_cp("parallel", "arbitrary"),
        name="attn_sample",
    )(qbd, cache, cache, kv_new, kv_new, tabc, tabn, jnp.asarray(mask))


def _outproj_even_kernel(*refs):
    ya_ref, yb_ref, x_ref, w_ref, g_ref, b_ref = refs[:6]
    o_ref, ot_ref = refs[-2:]
    y = jnp.dot(ya_ref[...].astype(BF16), w_ref[0:RG_WIDTH, :], preferred_element_type=F32)
    y = y + jnp.dot(yb_ref[...].astype(BF16), w_ref[RG_WIDTH:, :], preferred_element_type=F32)
    out = _ln_rows(ALPHA * x_ref[...] + y, g_ref[...], b_ref[...])
    o_ref[...] = out
    _store_row_tiles(ot_ref, out)


def _token_call(kernel_fn, name, n, x_off_rows, out_rows, out_off_rows, row_inputs, x_arr, fixed_inputs, prev):
    tm = _row_tile(n)
    nt = n // tm
    xo, oo = x_off_rows // tm, out_off_rows // tm
    tail = prev is None and out_rows > out_off_rows + n
    if tail:
        assert out_off_rows == 0 and out_rows - n <= tm
        body = kernel_fn

        def kernel_fn(*refs):
            step = pl.program_id(0)
            pl.when(step < nt)(lambda: body(*refs))

            @pl.when(step >= nt)
            def _():
                refs[-2][...] = jnp.zeros(refs[-2].shape, F32)
                refs[-1][...] = jnp.zeros(refs[-1].shape, F32)

    src = lambda i: jnp.minimum(i, nt - 1)
    in_specs = [pl.BlockSpec((tm, a.shape[1]), lambda i: (src(i), 0)) for a in row_inputs]
    in_specs.append(pl.BlockSpec((tm, D_MODEL), lambda i: (src(i) + xo, 0)))
    in_specs += [pl.BlockSpec(a.shape, lambda i, nd=a.ndim: (0,) * nd) for a in fixed_inputs]
    args = list(row_inputs) + [x_arr] + list(fixed_inputs)
    aliases = {}
    if prev is not None:
        in_specs += [pl.BlockSpec(memory_space=pl.ANY)] * 2
        aliases = {len(args): 0, len(args) + 1: 1}
        args += list(prev)
    return pl.pallas_call(
        kernel_fn,
        grid=(nt + int(tail),),
        in_specs=in_specs,
        out_specs=[pl.BlockSpec((tm, D_MODEL), lambda i: (i + oo, 0)),
                   pl.BlockSpec((tm * ROW_TILES, LANES), lambda i: (i + oo, 0))],
        out_shape=[jax.ShapeDtypeStruct((out_rows, D_MODEL), F32),
                   jax.ShapeDtypeStruct((out_rows * ROW_TILES, LANES), F32)],
        input_output_aliases=aliases,
        compiler_params=_cp("parallel"),
        name=name,
    )(*args)


def _inproj_odd_kernel(x_ref, w_ref, wdt_ref, z_ref, xbc_ref, dt_ref):
    xb = x_ref[...].astype(BF16)
    z_ref[...] = jnp.dot(xb, w_ref[:, 0:SSD_D_INNER], preferred_element_type=F32)
    xbc_ref[...] = jnp.dot(xb, w_ref[:, SSD_D_INNER:SSD_MAIN], preferred_element_type=F32)
    dt_ref[...] = jnp.dot(xb, wdt_ref[...], preferred_element_type=F32)


def _inproj_odd(x_all, row0, n, w, wdt):
    tm = _row_tile(n)
    off = row0 // tm
    widths = (SSD_D_INNER, SSD_CONV_DIM, LANES)
    return pl.pallas_call(
        _inproj_odd_kernel,
        grid=(n // tm,),
        in_specs=[pl.BlockSpec((tm, D_MODEL), lambda i: (i + off, 0)),
                  pl.BlockSpec((D_MODEL, SSD_MAIN), lambda i: (0, 0)),
                  pl.BlockSpec((D_MODEL, LANES), lambda i: (0, 0))],
        out_specs=[pl.BlockSpec((tm, c), lambda i: (i, 0)) for c in widths],
        out_shape=[jax.ShapeDtypeStruct((n, c), F32) for c in widths],
        compiler_params=_cp("parallel"),
        name="inproj_odd",
    )(x_all, w, wdt)


def _ssd_kernel(xbc_ref, dt_ref, cbuf_ref, h0_ref, cw_ref, cb_ref, dtb_ref, a_ref, dsk_ref,
                y_ref, cnew_ref, hT_ref, xx, dq, S, *, tt):
    c = pl.program_id(1)
    q = SSD_Q
    hist = CONV_W - 1
    base = 8

    @pl.when(c == 0)
    def _():
        if tt < q:
            xx[...] = jnp.zeros(xx.shape, F32)
            dq[...] = jnp.zeros(dq.shape, F32)
        xx[base - hist:base, :] = cbuf_ref[...]
        S[...] = h0_ref[...]

    @pl.when(c > 0)
    def _():
        xx[base - hist:base, :] = xx[base + tt - hist:base + tt, :]

    xx[base:base + tt, :] = xbc_ref[...]
    dq[0:tt, :] = dt_ref[...]

    @pl.when(c == pl.num_programs(1) - 1)
    def _():
        cnew_ref[...] = xx[base + tt - hist:base + tt, :]

    def conv_cols(lo, hi):
        acc = cb_ref[:, lo:hi]
        for k in range(CONV_W):
            acc = acc + cw_ref[k:k + 1, lo:hi] * _bf16_round(xx[base - hist + k:base - hist + k + q, lo:hi])
        return _silu(acc)

    row = lax.broadcasted_iota(I32, (q, q), 0)
    col = lax.broadcasted_iota(I32, (q, q), 1)
    causal = row >= col
    dt = _softplus(dq[...] + dtb_ref[...])
    if tt < q:
        dt = jnp.where(lax.broadcasted_iota(I32, (q, LANES), 0) < tt, dt, 0.0)
    da = dt * a_ref[...]
    cum = jnp.dot(causal.astype(F32), da, precision=lax.Precision.HIGHEST, preferred_element_type=F32)
    cum_t = cum.T
    dt_t = dt.T
    cum_last = cum[q - 1:q, :]
    ecum = jnp.exp(cum)
    dtde = dt * jnp.exp(cum_last - cum)
    cd = jnp.exp(cum_last)
    lo = lax.broadcasted_iota(I32, (q, LANES), 1) < SSD_P
    lo1 = lo[0:1, :]
    nt = (((1,), (1,)), ((), ()))
    pairs_per_group = SSD_HEADS // SSD_GROUPS // 2

    for g in range(SSD_GROUPS):
        bm = conv_cols(SSD_D_INNER + g * SSD_N, SSD_D_INNER + (g + 1) * SSD_N)
        cm = conv_cols(SSD_D_INNER + SSD_GN + g * SSD_N, SSD_D_INNER + SSD_GN + (g + 1) * SSD_N)
        bmb, cmb = bm.astype(BF16), cm.astype(BF16)
        cb = lax.dot_general(cmb, bmb, nt, preferred_element_type=F32)
        bm_t = bm.T.astype(BF16)
        for pp in range(pairs_per_group):
            p = g * pairs_per_group + pp
            h0, h1 = 2 * p, 2 * p + 1
            cs = slice(p * LANES, (p + 1) * LANES)
            xs = conv_cols(p * LANES, (p + 1) * LANES)
            xsb = xs.astype(BF16)
            yd = []
            for h in (h0, h1):
                seg = cum[:, h:h + 1] - cum_t[h:h + 1, :]
                m = cb * jnp.where(causal, jnp.exp(jnp.where(causal, seg, 0.0)), 0.0) * dt_t[h:h + 1, :]
                yd.append(jnp.dot(m.astype(BF16), xsb, preferred_element_type=F32))
            s_old = S[:, cs]
            yo = jnp.dot(cmb, s_old.astype(BF16), preferred_element_type=F32)
            yo = yo * jnp.where(lo, ecum[:, h0:h0 + 1], ecum[:, h1:h1 + 1])
            y = jnp.where(lo, yd[0], yd[1]) + yo + dsk_ref[:, cs] * xs
            y_ref[:, cs] = y[0:tt, :]
            wts = jnp.where(lo, dtde[:, h0:h0 + 1], dtde[:, h1:h1 + 1])
            upd = jnp.dot(bm_t, (xs * wts).astype(BF16), preferred_element_type=F32)
            S[:, cs] = s_old * jnp.where(lo1, cd[:, h0:h0 + 1], cd[:, h1:h1 + 1]) + upd

    @pl.when(c == pl.num_programs(1) - 1)
    def _():
        hT_ref[...] = S[...]


def _ssd(xbc, dt, cbuf, h0t, cw, cb, dtb, a, dsk):
    bsz, t, _ = xbc.shape
    tt = min(t, SSD_Q)
    row = lambda b, c: (b, c, 0)
    per_b = lambda b, c: (b, 0, 0)
    fixed = lambda b, c: (0, 0)
    return pl.pallas_call(
        functools.partial(_ssd_kernel, tt=tt),
        grid=(bsz, t // tt),
        in_specs=[pl.BlockSpec((None, tt, SSD_CONV_DIM), row), pl.BlockSpec((None, tt, LANES), row),
                  pl.BlockSpec((None, CONV_W - 1, SSD_CONV_DIM), per_b),
                  pl.BlockSpec((None, SSD_N, SSD_D_INNER), per_b),
                  pl.BlockSpec((CONV_W, SSD_CONV_DIM), fixed), pl.BlockSpec((1, SSD_CONV_DIM), fixed),
                  pl.BlockSpec((1, LANES), fixed), pl.BlockSpec((1, LANES), fixed),
                  pl.BlockSpec((1, SSD_D_INNER), fixed)],
        out_specs=[pl.BlockSpec((None, tt, SSD_D_INNER), row),
                   pl.BlockSpec((None, CONV_W - 1, SSD_CONV_DIM), per_b),
                   pl.BlockSpec((None, SSD_N, SSD_D_INNER), per_b)],
        out_shape=[jax.ShapeDtypeStruct((bsz, t, SSD_D_INNER), F32),
                   jax.ShapeDtypeStruct((bsz, CONV_W - 1, SSD_CONV_DIM), F32),
                   jax.ShapeDtypeStruct((bsz, SSD_N, SSD_D_INNER), F32)],
        scratch_shapes=[pltpu.VMEM((8 + SSD_Q, SSD_CONV_DIM), F32), pltpu.VMEM((SSD_Q, LANES), F32),
                        pltpu.VMEM((SSD_N, SSD_D_INNER), F32)],
        compiler_params=_cp("parallel", "arbitrary"),
        name="ssd",
    )(xbc, dt, cbuf, h0t, cw, cb, dtb, a, dsk)


def _outproj_odd_kernel(*refs):
    y_ref, z_ref, x_ref, nw_ref, w_ref, g_ref, b_ref = refs[:7]
    o_ref, ot_ref = refs[-2:]
    gw = SSD_D_INNER // SSD_GROUPS
    acc = None
    for g in range(SSD_GROUPS):
        cs = slice(g * gw, (g + 1) * gw)
        v = y_ref[:, cs] * _silu(z_ref[:, cs])
        v = v * lax.rsqrt(jnp.mean(v * v, -1, keepdims=True) + RMS_EPS) * nw_ref[:, cs]
        part = jnp.dot(v.astype(BF16), w_ref[cs, :], preferred_element_type=F32)
        acc = part if acc is None else acc + part
    out = _ln_rows(ALPHA * x_ref[...] + acc, g_ref[...], b_ref[...])
    o_ref[...] = out
    _store_row_tiles(ot_ref, out)


def _router_kernel(x_ref, w_ref, b_ref, idx_ref, gate_ref, rank_ref, cnt_ref, carry, *, tm):
    i = pl.program_id(0)

    @pl.when(i == 0)
    def _():
        carry[...] = jnp.zeros(carry.shape, F32)

    logits = jnp.dot(x_ref[...].astype(BF16), w_ref[...].astype(BF16), preferred_element_type=F32) + b_ref[...]
    lane = lax.broadcasted_iota(I32, (tm, N_EXPERTS), 1)
    work = logits
    vals, hots = [], []
    for _ in range(TOP_K):
        m = jnp.max(work, axis=1, keepdims=True)
        ik = jnp.min(jnp.where(work == m, lane, N_EXPERTS), axis=1, keepdims=True)
        hot = lane == ik
        vals.append(m)
        hots.append(hot)
        work = jnp.where(hot, NEG_INF, work)
    es = [jnp.exp(v - vals[0]) for v in vals]
    den = es[0] + es[1] + es[2] + es[3]
    hot_all = (hots[0] | hots[1] | hots[2] | hots[3])
    r_i = lax.broadcasted_iota(I32, (tm, tm), 0)
    c_i = lax.broadcasted_iota(I32, (tm, tm), 1)
    before = (r_i > c_i).astype(BF16)
    pre = jnp.dot(before, hot_all.astype(BF16), preferred_element_type=F32) + carry[...]
    lane4 = lax.broadcasted_iota(I32, (tm, TOP_K), 1)
    idx = jnp.zeros((tm, TOP_K), I32)
    gate = jnp.zeros((tm, TOP_K), F32)
    rank = jnp.zeros((tm, TOP_K), I32)
    for k in range(TOP_K):
        ik = jnp.sum(jnp.where(hots[k], lane, 0), axis=1, keepdims=True)
        rk = jnp.sum(jnp.where(hots[k], pre, 0.0), axis=1, keepdims=True).astype(I32)
        idx = jnp.where(lane4 == k, ik, idx)
        rank = jnp.where(lane4 == k, rk, rank)
        gate = jnp.where(lane4 == k, es[k] / den, gate)
    idx_ref[...] = idx
    gate_ref[...] = gate
    rank_ref[...] = rank
    carry[...] = carry[...] + jnp.sum(hot_all.astype(F32), axis=0, keepdims=True)
    cnt_ref[...] = carry[...]


def _router(x_all, w, b):
    n = x_all.shape[0]
    tm = 384 if n % 384 == 0 else 128
    out4 = lambda dt: jax.ShapeDtypeStruct((n, TOP_K), dt)
    return pl.pallas_call(
        functools.partial(_router_kernel, tm=tm),
        grid=(n // tm,),
        in_specs=[pl.BlockSpec((tm, D_MODEL), lambda i: (i, 0)),
                  pl.BlockSpec((D_MODEL, N_EXPERTS), lambda i: (0, 0)),
                  pl.BlockSpec((1, N_EXPERTS), lambda i: (0, 0))],
        out_specs=[pl.BlockSpec((tm, TOP_K), lambda i: (i, 0))] * 3 + [pl.BlockSpec((1, N_EXPERTS), lambda i: (0, 0))],
        out_shape=[out4(I32), out4(F32), out4(I32), jax.ShapeDtypeStruct((1, N_EXPERTS), F32)],
        scratch_shapes=[pltpu.VMEM((1, N_EXPERTS), F32)],
        compiler_params=_cp("arbitrary"),
        name="router",
    )(x_all, w, b)


def _tile_copy(src_hbm, row8, dst, dst_row8, sem):
    return pltpu.make_async_copy(src_hbm.at[pl.ds(pl.multiple_of(row8, ROW_TILES), ROW_TILES), :],
                                 dst.at[pl.ds(pl.multiple_of(dst_row8, ROW_TILES), ROW_TILES), :], sem)


def _dispatch_kernel(pos_ref, xt_ref, xs_init_hbm, xs_hbm, sem, *, tm):
    del xs_init_hbm
    i = pl.program_id(0)

    def issue(t, _):
        src = xt_ref.at[pl.ds(pl.multiple_of(t * ROW_TILES, ROW_TILES), ROW_TILES), :]
        for k in range(TOP_K):
            slot8 = pl.multiple_of(pos_ref[(i * tm + t) * TOP_K + k], ROW_TILES)
            pltpu.make_async_copy(src, xs_hbm.at[pl.ds(slot8, ROW_TILES), :], sem).start(priority=k % 2)
        return 0

    lax.fori_loop(0, tm, issue, 0, unroll=4)
    for _ in range(TOP_K):
        pltpu.make_async_copy(xt_ref, xs_hbm.at[pl.ds(0, tm * ROW_TILES), :], sem).wait()


def _dispatch(pos8_flat, xt_all, n_slots, slot_init):
    if slot_init is None:
        slot_init = jnp.zeros((n_slots * ROW_TILES, LANES), F32)
    n = xt_all.shape[0] // ROW_TILES
    tm = 128
    grid_spec = pltpu.PrefetchScalarGridSpec(
        num_scalar_prefetch=1,
        grid=(n // tm,),
        in_specs=[pl.BlockSpec((tm * ROW_TILES, LANES), lambda i, pos: (i, 0)),
                  pl.BlockSpec(memory_space=pl.ANY)],
        out_specs=pl.BlockSpec(memory_space=pl.ANY),
        scratch_shapes=[pltpu.SemaphoreType.DMA],
    )
    return pl.pallas_call(
        functools.partial(_dispatch_kernel, tm=tm),
        grid_spec=grid_spec,
        out_shape=jax.ShapeDtypeStruct((n_slots * ROW_TILES, LANES), F32),
        input_output_aliases={2: 0},
        compiler_params=_cp("arbitrary"),
        name="moe_dispatch",
    )(pos8_flat, xt_all, slot_init)


def _ffn_kernel(te_ref, nv_ref, xs_ref, w1_ref, b1_ref, w2_ref, b2_ref, ot_ref, w1b, w2i, w2b, *, rows):
    i = pl.program_id(0)
    nvalid = nv_ref[0]
    half = D_FF // 2

    @pl.when((i < nvalid) & ((i == 0) | (te_ref[i] != te_ref[jnp.maximum(i - 1, 0)])))
    def _():
        w1b[...] = w1_ref[...].astype(BF16)
        for c in range(ROW_TILES):
            cs = slice(c * LANES, (c + 1) * LANES)
            w2i[c, pl.ds(0, half, stride=2), :] = w2_ref[0:half, cs]
            w2i[c, pl.ds(1, half, stride=2), :] = w2_ref[half:D_FF, cs]
            w2b[:, cs] = w2i[c].astype(BF16)

    @pl.when(i < nvalid)
    def _():
        xb = _load_row_tiles(xs_ref, rows).astype(BF16)
        even = (lax.broadcasted_iota(I32, (rows, LANES), 1) % 2) == 0
        gw = 2 * LANES
        out = b2_ref[...]
        for gq in range(D_FF // gw):
            ca, cb = slice(gq * gw, (gq + 1) * gw), slice(D_FF + gq * gw, D_FF + (gq + 1) * gw)
            h_a = jnp.dot(xb, w1b[:, ca], preferred_element_type=F32) + b1_ref[:, ca]
            h_b = jnp.dot(xb, w1b[:, cb], preferred_element_type=F32) + b1_ref[:, cb]
            parts = []
            for c in range(gw // LANES):
                ha = h_a[:, c * LANES:(c + 1) * LANES]
                hb = h_b[:, c * LANES:(c + 1) * LANES]
                glu = jnp.minimum(jnp.where(even, ha, pltpu.roll(hb, 1, 1)), SWIGLU_LIMIT)
                lin = jnp.clip(jnp.where(even, pltpu.roll(ha, LANES - 1, 1), hb), -SWIGLU_LIMIT, SWIGLU_LIMIT)
                parts.append((glu * _sigmoid(SWIGLU_ALPHA * glu) * (lin + 1.0)).astype(BF16))
            out = out + jnp.dot(jnp.concatenate(parts, axis=1), w2b[ca, :], preferred_element_type=F32)
        _store_row_tiles(ot_ref, out)

    @pl.when(i >= nvalid)
    def _():
        ot_ref[...] = jnp.zeros(ot_ref.shape, F32)


def _ffn(xs_t, tile_expert, nvalid, layer, w1, b1, w2, b2, ntiles):
    rows = MOE_ROWS
    wspec = lambda shape: pl.BlockSpec((None, None) + shape, lambda i, te, nv: (layer, te[i], 0, 0))
    grid_spec = pltpu.PrefetchScalarGridSpec(
        num_scalar_prefetch=2,
        grid=(ntiles,),
        in_specs=[pl.BlockSpec((rows * ROW_TILES, LANES), lambda i, te, nv: (i, 0)),
                  wspec((D_MODEL, 2 * D_FF)), wspec((1, 2 * D_FF)),
                  wspec((D_FF, D_MODEL)), wspec((1, D_MODEL))],
        out_specs=pl.BlockSpec((rows * ROW_TILES, LANES), lambda i, te, nv: (i, 0)),
        scratch_shapes=[pltpu.VMEM((D_MODEL, 2 * D_FF), BF16), pltpu.VMEM((ROW_TILES, D_FF, LANES), F32),
                        pltpu.VMEM((D_FF, D_MODEL), BF16)],
    )
    return pl.pallas_call(
        functools.partial(_ffn_kernel, rows=rows),
        grid_spec=grid_spec,
        out_shape=jax.ShapeDtypeStruct((ntiles * rows * ROW_TILES, LANES), F32),
        compiler_params=_cp("arbitrary"),
        name="moe_ffn",
    )(tile_expert, nvalid, xs_t, w1, b1, w2, b2)


def _combine_kernel(pos_ref, ot_hbm, gate_ref, x_ref, g_ref, b_ref, out_ref, buf, sem, *, tm):
    i = pl.program_id(0)
    n = pl.num_programs(0)
    slot = i % 2

    def gather(tile, s):
        def issue(t, _):
            for k in range(TOP_K):
                _tile_copy(ot_hbm, pos_ref[(tile * tm + t) * TOP_K + k], buf.at[s, k], t * ROW_TILES,
                           sem.at[s]).start(priority=k % 2)
            return 0
        lax.fori_loop(0, tm, issue, 0, unroll=4)

    @pl.when(i == 0)
    def _():
        gather(0, 0)

    @pl.when(i + 1 < n)
    def _():
        gather(i + 1, 1 - slot)

    for k in range(TOP_K):
        pltpu.make_async_copy(ot_hbm.at[pl.ds(0, tm * ROW_TILES), :], buf.at[slot, k], sem.at[slot]).wait()
    f = None
    for k in range(TOP_K):
        part = gate_ref[:, k:k + 1] * _load_row_tiles(buf.at[slot, k], tm)
        f = part if f is None else f + part
    out_ref[...] = _ln_rows(ALPHA * x_ref[...] + f, g_ref[...], b_ref[...])


def _combine(pos_flat, ot_sorted, gate, x_all, g, b):
    n = x_all.shape[0]
    tm = 128
    grid_spec = pltpu.PrefetchScalarGridSpec(
        num_scalar_prefetch=1,
        grid=(n // tm,),
        in_specs=[pl.BlockSpec(memory_space=pl.ANY),
                  pl.BlockSpec((tm, TOP_K), lambda i, pos: (i, 0)),
                  pl.BlockSpec((tm, D_MODEL), lambda i, pos: (i, 0)),
                  pl.BlockSpec((1, D_MODEL), lambda i, pos: (0, 0)),
                  pl.BlockSpec((1, D_MODEL), lambda i, pos: (0, 0))],
        out_specs=pl.BlockSpec((tm, D_MODEL), lambda i, pos: (i, 0)),
        scratch_shapes=[pltpu.VMEM((2, TOP_K, tm * ROW_TILES, LANES), F32), pltpu.SemaphoreType.DMA((2,))],
    )
    return pl.pallas_call(
        functools.partial(_combine_kernel, tm=tm),
        grid_spec=grid_spec,
        out_shape=jax.ShapeDtypeStruct((n, D_MODEL), F32),
        compiler_params=_cp("arbitrary"),
        name="moe_combine",
    )(pos_flat, ot_sorted, gate, x_all, g, b)


def _moe(x_all, xt_all, layer, w_r, b_r, w1, b1, w2, b2, g, b, slot_init=None):
    n = x_all.shape[0]
    rows = MOE_ROWS
    ntiles = -(-(n * TOP_K + N_EXPERTS * (rows - 1)) // rows)
    idx, gate, rank, cnt = _router(x_all, w_r, b_r.reshape(1, N_EXPERTS))
    counts = cnt[0].astype(I32)
    padded = ((counts + rows - 1) // rows) * rows
    ends = jnp.cumsum(padded)
    starts = ends - padded
    experts = jnp.arange(N_EXPERTS, dtype=I32)
    pos = rank + jnp.sum(jnp.where(idx[:, :, None] == experts, starts, 0), axis=-1)
    tile_starts = jnp.arange(ntiles, dtype=I32) * rows
    tile_expert = jnp.minimum(jnp.sum((tile_starts[:, None] >= ends[None, :]).astype(I32), axis=1), N_EXPERTS - 1)
    nvalid = (ends[-1:] // rows).astype(I32)
    pos8 = pos.reshape(-1) * ROW_TILES
    xs_t = _dispatch(pos8, xt_all, ntiles * rows, slot_init)
    ot_sorted = _ffn(xs_t, tile_expert, nvalid, layer, w1, b1[:, :, None, :], w2, b2[:, :, None, :], ntiles)
    return _combine(pos8, ot_sorted, gate, x_all, g.reshape(1, -1), b.reshape(1, -1)), ot_sorted


def _block_diag(w):
    nb, bw, _ = w.shape
    eye = jnp.eye(nb, dtype=w.dtype)
    return (eye[:, None, :, None] * w[:, :, None, :]).reshape(nb * bw, nb * bw)


def _mix_even(x_arr, out_row0, n_total, bsz, t, rg_conv, rg_h, cache, p, prev):
    n = bsz * t
    xa, ga, q, kv = _inproj_even(x_arr, 0, n, p["w_in"])
    ya, conv_new, h_last = _rglru(
        xa.reshape(bsz, t, RG_WIDTH), ga.reshape(bsz, t, RG_WIDTH), rg_conv, rg_h.reshape(bsz, 1, RG_WIDTH),
        p["rg_conv_w"], p["rg_conv_b"], p["rg_wg"], p["rg_bg"], p["rg_c"])
    q3 = q.reshape(bsz, t, B_WIDTH)
    kv3 = kv.reshape(bsz, t, 2 * B_WIDTH)
    if cache is None:
        yb = _attn_prompt(q3, kv3, p["rel_bias"])
    else:
        yb = _attn_sample(q3, cache, kv3, p["rel_bias"])
    x1 = _token_call(_outproj_even_kernel, "outproj_even", n, 0, n_total, out_row0,
                     [ya.reshape(n, RG_WIDTH), yb.reshape(n, B_WIDTH)], x_arr,
                     [p["w_out"], p["ln_g0"], p["ln_b0"]], prev)
    return x1, conv_new, h_last.reshape(bsz, RG_WIDTH), kv3


def _mix_odd(x_all, row0, bsz, t, ssd_conv, ssd_h, p, prev):
    n = bsz * t
    z, xbc, dt = _inproj_odd(x_all, row0, n, p["ssd_w_main"], p["ssd_w_dt"])
    h0t = jnp.transpose(ssd_h.astype(F32), (0, 3, 1, 2)).reshape(bsz, SSD_N, SSD_D_INNER)
    y, conv_new, h_t = _ssd(xbc.reshape(bsz, t, SSD_CONV_DIM), dt.reshape(bsz, t, LANES), ssd_conv, h0t,
                            p["ssd_conv_w"], p["ssd_conv_b"], p["ssd_dtb"], p["ssd_a"], p["ssd_dsk"])
    h_last = jnp.transpose(h_t.reshape(bsz, SSD_N, SSD_HEADS, SSD_P), (0, 2, 3, 1))
    x1 = _token_call(_outproj_odd_kernel, "outproj_odd", n, row0, x_all.shape[0], row0,
                     [y.reshape(n, SSD_D_INNER), z], x_all,
                     [p["ssd_norm_w"], p["ssd_w_out"], p["ln_g2"], p["ln_b2"]], prev)
    return x1, conv_new, h_last


def _mixer_params(rel_bias, w_in_mix, rg_conv_w, rg_conv_b, rg_w_a, rg_b_a, rg_w_i, rg_b_i, rg_lambda, w_out_mix,
                  ssd_w_in, ssd_conv_w, ssd_conv_b, ssd_dt_bias, ssd_a_log, ssd_d, ssd_norm_w, ssd_w_out, ln_g, ln_b):
    row2 = lambda v: v.reshape(1, -1).astype(F32)
    pad_heads = lambda v: jnp.pad(v.astype(F32), (0, LANES - SSD_HEADS)).reshape(1, LANES)
    return dict(
        rel_bias=rel_bias,
        w_in=w_in_mix[0].astype(BF16),
        rg_conv_w=rg_conv_w[0].astype(F32), rg_conv_b=row2(rg_conv_b[0]),
        rg_wg=jnp.concatenate([_block_diag(rg_w_a[0]), _block_diag(rg_w_i[0])], axis=1).astype(BF16),
        rg_bg=jnp.concatenate([rg_b_a[0].reshape(1, -1), rg_b_i[0].reshape(1, -1)], axis=1).astype(F32),
        rg_c=row2(-RG_C * jax.nn.softplus(-rg_lambda[0].astype(F32))),
        w_out=w_out_mix[0].astype(BF16),
        ln_g0=row2(ln_g[0, 0]), ln_b0=row2(ln_b[0, 0]),
        ssd_w_main=ssd_w_in[0][:, :SSD_MAIN].astype(BF16),
        ssd_w_dt=jnp.pad(ssd_w_in[0][:, SSD_MAIN:], ((0, 0), (0, LANES - SSD_HEADS))).astype(BF16),
        ssd_conv_w=ssd_conv_w[0].astype(F32), ssd_conv_b=row2(ssd_conv_b[0]),
        ssd_dtb=pad_heads(ssd_dt_bias[0]), ssd_a=pad_heads(-jnp.exp(ssd_a_log[0].astype(F32))),
        ssd_dsk=row2(jnp.repeat(ssd_d[0].astype(F32), SSD_P)),
        ssd_norm_w=row2(ssd_norm_w[0]), ssd_w_out=ssd_w_out[0].astype(BF16),
        ln_g2=row2(ln_g[1, 0]), ln_b2=row2(ln_b[1, 0]),
    )


def kernel(x_prompt, x_sample, state_rglru_conv, state_rglru_h, cache_swa_kv, state_ssd_conv, state_ssd_h,
           rel_bias, w_in_mix, rg_conv_w, rg_conv_b, rg_w_a, rg_b_a, rg_w_i, rg_b_i, rg_lambda, w_out_mix,
           ssd_w_in, ssd_conv_w, ssd_conv_b, ssd_dt_bias, ssd_a_log, ssd_d, ssd_norm_w, ssd_w_out,
           ln_g, ln_b, router_w, router_b, exp_w1, exp_b1, exp_w2, exp_b2):
    bp, tp, _ = x_prompt.shape
    bs, ts, _ = x_sample.shape
    n_p, n_s = bp * tp, bs * ts
    n_total = n_p + n_s
    p = _mixer_params(rel_bias, w_in_mix, rg_conv_w, rg_conv_b, rg_w_a, rg_b_a, rg_w_i, rg_b_i, rg_lambda, w_out_mix,
                      ssd_w_in, ssd_conv_w, ssd_conv_b, ssd_dt_bias, ssd_a_log, ssd_d, ssd_norm_w, ssd_w_out, ln_g, ln_b)

    xp = x_prompt.reshape(n_p, D_MODEL)
    xs = x_sample.reshape(n_s, D_MODEL)
    zeros = lambda *s: jnp.zeros(s, F32)
    x1, p_rg_conv, p_rg_h, p_kv = _mix_even(xp, 0, n_total, bp, tp, zeros(bp, CONV_W - 1, RG_WIDTH),
                                            zeros(bp, RG_WIDTH), None, p, None)
    cache = cache_swa_kv[0].reshape(bs, cache_swa_kv.shape[2], 2 * B_WIDTH)
    x1, s_rg_conv, s_rg_h, s_kv = _mix_even(xs, n_p, n_total, bs, ts, state_rglru_conv[0], state_rglru_h[0],
                                            cache, p, x1)
    x2, spent_slots = _moe(x1[0], x1[1], 0, router_w[0].astype(F32), router_b[0].astype(F32), exp_w1, exp_b1, exp_w2,
                           exp_b2, ln_g[0, 1], ln_b[0, 1])

    x3, p_ssd_conv, p_ssd_h = _mix_odd(x2, 0, bp, tp, zeros(bp, CONV_W - 1, SSD_CONV_DIM),
                                       zeros(bp, SSD_HEADS, SSD_P, SSD_N), p, None)
    x3, s_ssd_conv, s_ssd_h = _mix_odd(x2, n_p, bs, ts, state_ssd_conv[0], state_ssd_h[0], p, x3)
    x4, _ = _moe(x3[0], x3[1], 1, router_w[1].astype(F32), router_b[1].astype(F32), exp_w1, exp_b1, exp_w2, exp_b2,
                 ln_g[1, 1], ln_b[1, 1], slot_init=spent_slots)

    kv_shape = lambda b, t: (1, b, t, 2, B_HEADS, B_HD)
    return (x4[:n_p].reshape(bp, tp, D_MODEL), x4[n_p:].reshape(bs, ts, D_MODEL),
            p_rg_conv[None], p_rg_h[None], p_kv[:, -min(MAX_WINDOW, tp):].reshape(kv_shape(bp, min(MAX_WINDOW, tp))),
            p_ssd_conv[None], p_ssd_h[None],
            s_rg_conv[None], s_rg_h[None], s_kv.reshape(kv_shape(bs, ts)),
            s_ssd_conv[None], s_ssd_h[None])
```
